```python
import math
import jax, jax.numpy as jnp
from jax import lax
import numpy as np

D_MODEL = 1024
BATCH = 4
SEQ = 4096
DEPTH = 4

CTX_LEN = 256
GRID_W = 64
POOL_WIDTH = 512
POOL_WINDOWS = (2, 4, 8, 16)
POOL_GROUPS = len(POOL_WINDOWS)
POOL_GC = POOL_WIDTH // POOL_GROUPS
N_HEADS = 4
HEAD_DIM = 64
V_DIM = 2 * HEAD_DIM
QK_WIDTH = N_HEADS * 2 * HEAD_DIM
ATTN_WIDTH = N_HEADS * V_DIM
MIX_WIDTH = POOL_WIDTH + ATTN_WIDTH
IN_WIDTH = POOL_WIDTH + 2 * QK_WIDTH + ATTN_WIDTH
ROT_AXIS = HEAD_DIM // 2
ROPE_BASE = 10000.0
D_FF = 4 * D_MODEL
Q_BLOCK = 128
N_ADA = 6
EPS = 1e-6

kernel_name = "hybrid_pool_diffattn_dit_block"


def rms_norm(x, g):
    xf = x.astype(jnp.float32)
    y = xf * lax.rsqrt(jnp.mean(xf * xf, axis=-1, keepdims=True) + EPS)
    return (y * g.astype(jnp.float32)).astype(x.dtype)


def modulate(h, shift, scale):
    return h * (1 + scale) + shift


def ada_params(cvec, w_ada, b_ada):
    return jnp.split(jax.nn.silu(cvec) @ w_ada + b_ada, N_ADA, axis=-1)


def axial_rope(n, dtype):
    rows = n // GRID_W
    row = jnp.repeat(jnp.arange(rows), GRID_W).astype(jnp.float32)
    col = jnp.tile(jnp.arange(GRID_W), rows).astype(jnp.float32)
    inv = ROPE_BASE ** (-jnp.arange(0, ROT_AXIS, 2, dtype=jnp.float32) / ROT_AXIS)
    ang = jnp.concatenate([row[:, None] * inv, col[:, None] * inv], axis=-1)
    return jnp.cos(ang).astype(dtype), jnp.sin(ang).astype(dtype)


def apply_rope(x, cos, sin):
    x1, x2 = x[..., :HEAD_DIM // 2], x[..., HEAD_DIM // 2:]
    return jnp.concatenate([x1 * cos - x2 * sin, x1 * sin + x2 * cos], axis=-1)


def centred_mean(x, w):
    b, n, ch = x.shape
    xf = x.astype(jnp.float32)
    cs = jnp.concatenate([jnp.zeros((b, 1, ch), jnp.float32), jnp.cumsum(xf, axis=1)], axis=1)
    t = jnp.arange(n)
    lo = jnp.clip(t - w // 2, 0, n)
    hi = jnp.clip(t + w - w // 2, 0, n)
    s = cs[:, hi] - cs[:, lo]
    cnt = (hi - lo).astype(jnp.float32)
    return (s / cnt[None, :, None]).astype(x.dtype)


def pool_mixer(u, w_pool, s_pool):
    b, n, _ = u.shape
    groups = jnp.split(u, POOL_GROUPS, axis=-1)
    y = jnp.stack([centred_mean(g, w) - g for g, w in zip(groups, POOL_WINDOWS)], axis=2)
    y = jnp.einsum('bngc,gcd->bngd', y, w_pool).reshape(b, n, POOL_WIDTH)
    return y * s_pool


def split_heads(z):
    b, n, _ = z.shape
    u, q, k, v = jnp.split(z, [POOL_WIDTH, POOL_WIDTH + QK_WIDTH, POOL_WIDTH + 2 * QK_WIDTH], axis=-1)
    q = q.reshape(b, n, 2 * N_HEADS, HEAD_DIM).transpose(0, 2, 1, 3)
    k = k.reshape(b, n, 2 * N_HEADS, HEAD_DIM).transpose(0, 2, 1, 3)
    v = v.reshape(b, n, N_HEADS, V_DIM).transpose(0, 2, 1, 3)
    return u, q, k, v


def diff_attn(q, k, v, lam):
    b, _, nq, _ = q.shape
    nk = k.shape[2]
    s = jnp.einsum('bhqd,bhkd->bhqk', q, k).astype(jnp.float32) * (HEAD_DIM ** -0.5)
    a = jax.nn.softmax(s, axis=-1).reshape(b, N_HEADS, 2, nq, nk)
    w = a[:, :, 0] - lam * a[:, :, 1]
    return jnp.einsum('bhqk,bhkd->bhqd', w.astype(v.dtype), v)


def diff_attn_blocks(q, k, v, lam):
    b, hh, n, d = q.shape
    nb = n // Q_BLOCK
    qb = q.reshape(b, hh, nb, Q_BLOCK, d).transpose(2, 0, 1, 3, 4)
    o = lax.map(lambda qblk: diff_attn(qblk, k, v, lam), qb)
    return o.transpose(1, 2, 0, 3, 4).reshape(b, N_HEADS, n, V_DIM)


def head_out(o, g_sub, lam_init):
    o = rms_norm(o, g_sub) * (1.0 - lam_init)
    b, h, n, d = o.shape
    return o.transpose(0, 2, 1, 3).reshape(b, n, h * d)


def sq_relu_mlp(h, w1, w2):
    return jnp.square(jax.nn.relu(h @ w1)) @ w2


def setup_inputs(seed: int = 0) -> dict:
    key = jax.random.key(seed)
    ks = jax.random.split(key, 24)
    f = jnp.float32
    nrm = lambda k, shape, s: jax.random.normal(k, shape, f) * s
    L = DEPTH
    return {
        "x": nrm(ks[0], (BATCH, SEQ, D_MODEL), 1.0),
        "c": nrm(ks[1], (BATCH, D_MODEL), 1.0),
        "ctx": nrm(ks[2], (BATCH, CTX_LEN, D_MODEL), 1.0),
        "c_ctx": nrm(ks[3], (D_MODEL,), 1.0),
        "w_ada": nrm(ks[4], (L, D_MODEL, N_ADA * D_MODEL), 0.02),
        "b_ada": nrm(ks[5], (L, N_ADA * D_MODEL), 0.01),
        "g_mix": 1.0 + nrm(ks[6], (L, D_MODEL), 0.02),
        "g_mlp": 1.0 + nrm(ks[7], (L, D_MODEL), 0.02),
        "w_in": nrm(ks[8], (L, D_MODEL, IN_WIDTH), D_MODEL ** -0.5),
        "w_pool": nrm(ks[9], (L, POOL_GROUPS, POOL_GC, POOL_GC), POOL_GC ** -0.5),
        "s_pool": 1.0 + nrm(ks[10], (L, POOL_WIDTH), 0.1),
        "lam_q1": nrm(ks[11], (L, HEAD_DIM), 0.1),
        "lam_k1": nrm(ks[12], (L, HEAD_DIM), 0.1),
        "lam_q2": nrm(ks[13], (L, HEAD_DIM), 0.1),
        "lam_k2": nrm(ks[14], (L, HEAD_DIM), 0.1),
        "g_subln": 1.0 + nrm(ks[15], (L, V_DIM), 0.02),
        "w_out": nrm(ks[16], (L, MIX_WIDTH, D_MODEL), MIX_WIDTH ** -0.5),
        "w_mlp1": nrm(ks[17], (L, D_MODEL, D_FF), D_MODEL ** -0.5),
        "w_mlp2": nrm(ks[18], (L, D_FF, D_MODEL), D_FF ** -0.5),
        "g_final": 1.0 + nrm(ks[19], (D_MODEL,), 0.02),
    }


def reference(x, c, ctx, c_ctx, w_ada, b_ada, g_mix, g_mlp, w_in, w_pool, s_pool,
              lam_q1, lam_k1, lam_q2, lam_k2, g_subln, w_out, w_mlp1, w_mlp2, g_final):
    n = x.shape[1]
    cos, sin = axial_rope(n, x.dtype)
    c_lat = c[:, None, :]
    c_cx = c_ctx[None, None, :]
    for l in range(DEPTH):
        last = l == DEPTH - 1
        sa, ca, ga, sm, cm, gm = ada_params(c_lat, w_ada[l], b_ada[l])
        sa_c, ca_c, ga_c, sm_c, cm_c, gm_c = ada_params(c_cx, w_ada[l], b_ada[l])
        lam_init = 0.8 - 0.6 * math.exp(-0.3 * l)
        lam = (jnp.exp(jnp.sum(lam_q1[l].astype(jnp.float32) * lam_k1[l].astype(jnp.float32)))
               - jnp.exp(jnp.sum(lam_q2[l].astype(jnp.float32) * lam_k2[l].astype(jnp.float32)))
               + lam_init)

        h = modulate(rms_norm(x, g_mix[l]), sa, ca)
        hc = modulate(rms_norm(ctx, g_mix[l]), sa_c, ca_c)
        u, q, k, v = split_heads(h @ w_in[l])
        uc, qc, kc, vc = split_heads(hc @ w_in[l])
        q = apply_rope(q, cos, sin)
        k = apply_rope(k, cos, sin)
        k_all = jnp.concatenate([kc, k], axis=2)
        v_all = jnp.concatenate([vc, v], axis=2)
        o_att = diff_attn_blocks(q, k_all, v_all, lam)
        mix = jnp.concatenate([pool_mixer(u, w_pool[l], s_pool[l]),
                               head_out(o_att, g_subln[l], lam_init)], axis=-1)
        x = x + ga * (mix @ w_out[l])

        x = x + gm * sq_relu_mlp(modulate(rms_norm(x, g_mlp[l]), sm, cm), w_mlp1[l], w_mlp2[l])

        if not last:
            o_c = diff_attn(qc, kc, vc, lam)
            mix_c = jnp.concatenate([pool_mixer(uc, w_pool[l], s_pool[l]),
                                     head_out(o_c, g_subln[l], lam_init)], axis=-1)
            ctx = ctx + ga_c * (mix_c @ w_out[l])
            ctx = ctx + gm_c * sq_relu_mlp(modulate(rms_norm(ctx, g_mlp[l]), sm_c, cm_c),
                                           w_mlp1[l], w_mlp2[l])
    return rms_norm(x, g_final)
```

```python
import functools
import math

import jax
import jax.numpy as jnp
from jax import lax
from jax.experimental import pallas as pl
from jax.experimental.pallas import tpu as pltpu

D_MODEL = 1024
GRID_W = 64
POOL_WIDTH = 512
POOL_WINDOWS = (2, 4, 8, 16)
POOL_GC = 128
N_HEADS = 4
HEAD_DIM = 64
V_DIM = 128
QK_WIDTH = 512
ATTN_WIDTH = 512
IN_WIDTH = 2048
ROPE_BASE = 10000.0
D_FF = 4096
N_ADA = 6
EPS = 1e-6

LANES = 128
ADA_ROWS = 8
ROW_TILE = 256
HALO = 16
KV_CHUNK = 512
FF_CHUNK = 1024
ADA_COLS = 1024
VMEM_LIMIT = 56 * 1024 * 1024

F32 = jnp.float32
BF16 = jnp.bfloat16


def _params(*sem):
    return pltpu.CompilerParams(dimension_semantics=sem, vmem_limit_bytes=VMEM_LIMIT)


def _ada_kernel(c_ref, w_ref, b_ref, lq1_ref, lk1_ref, lq2_ref, lk2_ref, linit_ref,
                mod_ref, lam_ref):
    c = c_ref[...]
    s = c / (1.0 + jnp.exp(-c))
    mod_ref[0] = jnp.dot(s, w_ref[0], preferred_element_type=F32,
                         precision=lax.Precision.HIGHEST) + b_ref[0]
    d1 = jnp.sum(lq1_ref[0] * lk1_ref[0], axis=-1, keepdims=True)
    d2 = jnp.sum(lq2_ref[0] * lk2_ref[0], axis=-1, keepdims=True)
    lam = jnp.exp(d1) - jnp.exp(d2) + linit_ref[0]
    lam_ref[0] = jnp.broadcast_to(lam, (ADA_ROWS, LANES))


def _ada_call(cvec, w_ada, b_ada, lq1, lk1, lq2, lk2, linit):
    depth = w_ada.shape[0]
    width = w_ada.shape[2]
    vec = lambda: pl.BlockSpec((1, 1, HEAD_DIM), lambda l, j: (l, 0, 0))
    return pl.pallas_call(
        _ada_kernel,
        grid=(depth, width // ADA_COLS),
        in_specs=[
            pl.BlockSpec((ADA_ROWS, D_MODEL), lambda l, j: (0, 0)),
            pl.BlockSpec((1, D_MODEL, ADA_COLS), lambda l, j: (l, 0, j)),
            pl.BlockSpec((1, 1, ADA_COLS), lambda l, j: (l, 0, j)),
            vec(), vec(), vec(), vec(),
            pl.BlockSpec((1, 1, LANES), lambda l, j: (l, 0, 0)),
        ],
        out_specs=[
            pl.BlockSpec((1, ADA_ROWS, ADA_COLS), lambda l, j: (l, 0, j)),
            pl.BlockSpec((1, ADA_ROWS, LANES), lambda l, j: (l, 0, 0)),
        ],
        out_shape=[
            jax.ShapeDtypeStruct((depth, ADA_ROWS, width), F32),
            jax.ShapeDtypeStruct((depth, ADA_ROWS, LANES), F32),
        ],
        compiler_params=_params("arbitrary", "arbitrary"),
        name="ada",
    )(cvec, w_ada, b_ada, lq1, lk1, lq2, lk2, linit)


def _mod_spec(batch, lat_tiles):
    return pl.BlockSpec((1, N_ADA, D_MODEL),
                        lambda b, i: (jnp.where(i >= lat_tiles, batch, b), 0, 0))


def _norm_mod(x, g, shift, scale):
    ms = jnp.mean(x * x, axis=-1, keepdims=True)
    return (x * lax.rsqrt(ms + EPS) * g) * (1.0 + scale) + shift


def _inproj_kernel(x_ref, mod_ref, g_ref, w_ref, cos_ref, sin_ref,
                   u_ref, q_ref, k_ref, v_ref):
    h = _norm_mod(x_ref[0], g_ref[...], mod_ref[0, 0:1, :], mod_ref[0, 1:2, :])
    z = jnp.dot(h.astype(BF16), w_ref[...], preferred_element_type=F32)
    u_ref[0] = z[:, :POOL_WIDTH]
    cos = cos_ref[...]
    sin = sin_ref[...]
    lane = lax.broadcasted_iota(jnp.int32, cos.shape, 1)
    first_half = (lane & (HEAD_DIM // 2)) == 0
    q_scale = HEAD_DIM ** -0.5

    def rope(t):
        partner = jnp.where(first_half,
                            pltpu.roll(t, LANES - HEAD_DIM // 2, 1),
                            pltpu.roll(t, HEAD_DIM // 2, 1))
        return t * cos + partner * sin

    for c in range(QK_WIDTH // LANES):
        lo = POOL_WIDTH + c * LANES
        q_ref[0, :, c * LANES:(c + 1) * LANES] = (rope(z[:, lo:lo + LANES]) * q_scale).astype(BF16)
        lo += QK_WIDTH
        k_ref[0, :, c * LANES:(c + 1) * LANES] = rope(z[:, lo:lo + LANES]).astype(BF16)
    v_ref[0] = z[:, POOL_WIDTH + 2 * QK_WIDTH:].astype(BF16)


def _inproj_call(xs, mods, g, w_in, cos, sin, lat_tiles):
    batch, rows, _ = xs.shape
    tile = lambda width: pl.BlockSpec((1, ROW_TILE, width), lambda b, i: (b, i, 0))
    table = pl.BlockSpec((ROW_TILE, LANES), lambda b, i: (i, 0))
    out = lambda dt: jax.ShapeDtypeStruct((batch, rows, POOL_WIDTH), dt)
    return pl.pallas_call(
        _inproj_kernel,
        grid=(batch, rows // ROW_TILE),
        in_specs=[
            tile(D_MODEL),
            _mod_spec(batch, lat_tiles),
            pl.BlockSpec((1, D_MODEL), lambda b, i: (0, 0)),
            pl.BlockSpec((D_MODEL, IN_WIDTH), lambda b, i: (0, 0)),
            table, table,
        ],
        out_specs=[tile(POOL_WIDTH)] * 4,
        out_shape=[out(F32), out(BF16), out(BF16), out(BF16)],
        compiler_params=_params("arbitrary", "arbitrary"),
        name="inproj",
    )(xs, mods, g, w_in, cos, sin)


def _attn_kernel(scal_ref, q_ref, k_ref, v_ref, g_ref, o_ref, *, n_lat, n_ctx):
    i = pl.program_id(2)
    tq = q_ref.shape[1]
    q = q_ref[0]
    lane = lax.broadcasted_iota(jnp.int32, q.shape, 1)
    zero = jnp.zeros_like(q)
    qq = jnp.concatenate([jnp.where(lane < HEAD_DIM, q, zero),
                          jnp.where(lane >= HEAD_DIM, q, zero)], axis=0)

    def step(carry, start, size):
        m, l, acc = carry
        kc = k_ref[0, pl.ds(start, size), :]
        vc = v_ref[0, pl.ds(start, size), :]
        s = lax.dot_general(qq, kc, (((1,), (1,)), ((), ())), preferred_element_type=F32)
        m_new = jnp.maximum(m, jnp.max(s, axis=-1, keepdims=True))
        alpha = jnp.exp(m - m_new)
        p = jnp.exp(s - m_new)
        l = alpha * l + jnp.sum(p, axis=-1, keepdims=True)
        acc = alpha * acc + jnp.dot(p.astype(BF16), vc, preferred_element_type=F32)
        return m_new, l, acc

    init = (jnp.full((2 * tq, 1), -jnp.inf, F32),
            jnp.zeros((2 * tq, 1), F32),
            jnp.zeros((2 * tq, V_DIM), F32))
    lat_chunks = jnp.where(i * tq < n_lat, n_lat // KV_CHUNK, 0)
    carry = lax.fori_loop(
        0, lat_chunks,
        lambda j, c: step(c, pl.multiple_of(j * KV_CHUNK, KV_CHUNK), KV_CHUNK),
        init)
    m, l, acc = step(carry, n_lat, n_ctx)
    o = acc / l
    o = o[:tq] - scal_ref[0] * o[tq:]
    ms = jnp.mean(o * o, axis=-1, keepdims=True)
    o_ref[0] = (o * lax.rsqrt(ms + EPS) * g_ref[...] * scal_ref[1]).astype(o_ref.dtype)


def _attn_call(scal, q, k, v, g, n_lat):
    batch, rows, _ = q.shape
    qo = pl.BlockSpec((1, ROW_TILE, V_DIM), lambda b, h, i: (b, i, h))
    kv = pl.BlockSpec((1, rows, V_DIM), lambda b, h, i: (b, 0, h))
    return pl.pallas_call(
        functools.partial(_attn_kernel, n_lat=n_lat, n_ctx=rows - n_lat),
        grid=(batch, N_HEADS, rows // ROW_TILE),
        in_specs=[
            pl.BlockSpec(memory_space=pltpu.SMEM),
            qo, kv, kv,
            pl.BlockSpec((1, V_DIM), lambda b, h, i: (0, 0)),
        ],
        out_specs=qo,
        out_shape=jax.ShapeDtypeStruct((batch, rows, ATTN_WIDTH), BF16),
        compiler_params=_params("arbitrary", "arbitrary", "arbitrary"),
        name="attn",
    )(scal, q, k, v, g)


def _mix_kernel(x_ref, u_ref, up_ref, un_ref, att_ref, mod_ref, band_ref, wp_ref, sp_ref,
                wo_ref, o_ref, *, n_lat, n_all):
    r0 = pl.program_id(1) * ROW_TILE
    in_ctx = r0 >= n_lat
    seg_lo = jnp.where(in_ctx, n_lat, 0)
    seg_hi = jnp.where(in_ctx, n_all, n_lat)
    cur = u_ref[0]
    ext = jnp.concatenate([up_ref[0], cur, un_ref[0]], axis=0)
    srow = r0 - HALO + lax.broadcasted_iota(jnp.int32, (ROW_TILE + 2 * HALO, 1), 0)
    ext = jnp.where((srow >= seg_lo) & (srow < seg_hi), ext, 0.0).astype(BF16)
    trow = r0 + lax.broadcasted_iota(jnp.int32, (ROW_TILE, 1), 0)
    mix = jnp.dot(att_ref[0], wo_ref[POOL_WIDTH:, :], preferred_element_type=F32)
    for gi, w in enumerate(POOL_WINDOWS):
        cols = slice(gi * POOL_GC, (gi + 1) * POOL_GC)
        tot = jnp.dot(band_ref[gi], ext[:, cols], preferred_element_type=F32)
        cnt = (jnp.minimum(trow + (w - w // 2), seg_hi) - jnp.maximum(trow - w // 2, seg_lo))
        y = tot / cnt.astype(F32) - cur[:, cols]
        p = jnp.dot(y.astype(BF16), wp_ref[gi], preferred_element_type=F32) * sp_ref[:, cols]
        mix += jnp.dot(p.astype(BF16), wo_ref[cols, :], preferred_element_type=F32)
    o_ref[0] = x_ref[0] + mod_ref[0, 2:3, :] * mix


def _mix_call(xs, u, att, mods, band, w_pool, s_pool, w_out, n_lat, lat_tiles):
    batch, rows, _ = xs.shape
    halo_per_tile = ROW_TILE // HALO
    last_halo = rows // HALO - 1
    tile = lambda width: pl.BlockSpec((1, ROW_TILE, width), lambda b, i: (b, i, 0))
    return pl.pallas_call(
        functools.partial(_mix_kernel, n_lat=n_lat, n_all=rows),
        grid=(batch, rows // ROW_TILE),
        in_specs=[
            tile(D_MODEL),
            tile(POOL_WIDTH),
            pl.BlockSpec((1, HALO, POOL_WIDTH),
                         lambda b, i: (b, jnp.maximum(i * halo_per_tile - 1, 0), 0)),
            pl.BlockSpec((1, HALO, POOL_WIDTH),
                         lambda b, i: (b, jnp.minimum((i + 1) * halo_per_tile, last_halo), 0)),
            tile(ATTN_WIDTH),
            _mod_spec(batch, lat_tiles),
            pl.BlockSpec(band.shape, lambda b, i: (0, 0, 0)),
            pl.BlockSpec(w_pool.shape, lambda b, i: (0, 0, 0)),
            pl.BlockSpec((1, POOL_WIDTH), lambda b, i: (0, 0)),
            pl.BlockSpec(w_out.shape, lambda b, i: (0, 0)),
        ],
        out_specs=tile(D_MODEL),
        out_shape=jax.ShapeDtypeStruct(xs.shape, F32),
        compiler_params=_params("arbitrary", "arbitrary"),
        name="mix",
    )(xs, u, u, u, att, mods, band, w_pool, s_pool, w_out)


def _mlp_kernel(x_ref, mod_ref, g_ref, w1_ref, w2_ref, o_ref):
    x = x_ref[0]
    h = _norm_mod(x, g_ref[...], mod_ref[0, 3:4, :], mod_ref[0, 4:5, :]).astype(BF16)
    acc = jnp.zeros(x.shape, F32)
    for c in range(D_FF // FF_CHUNK):
        cols = slice(c * FF_CHUNK, (c + 1) * FF_CHUNK)
        a = jnp.dot(h, w1_ref[:, cols], preferred_element_type=F32)
        a = jnp.square(jnp.maximum(a, 0.0)).astype(BF16)
        acc += jnp.dot(a, w2_ref[cols, :], preferred_element_type=F32)
    o_ref[0] = x + mod_ref[0, 5:6, :] * acc


def _mlp_call(xs, mods, g, w1, w2, lat_tiles):
    batch, rows, _ = xs.shape
    tile = pl.BlockSpec((1, ROW_TILE, D_MODEL), lambda b, i: (b, i, 0))
    return pl.pallas_call(
        _mlp_kernel,
        grid=(batch, rows // ROW_TILE),
        in_specs=[
            tile,
            _mod_spec(batch, lat_tiles),
            pl.BlockSpec((1, D_MODEL), lambda b, i: (0, 0)),
            pl.BlockSpec(w1.shape, lambda b, i: (0, 0)),
            pl.BlockSpec(w2.shape, lambda b, i: (0, 0)),
        ],
        out_specs=tile,
        out_shape=jax.ShapeDtypeStruct(xs.shape, F32),
        compiler_params=_params("arbitrary", "arbitrary"),
        name="mlp",
    )(xs, mods, g, w1, w2)


def _final_kernel(x_ref, g_ref, o_ref):
    x = x_ref[0]
    ms = jnp.mean(x * x, axis=-1, keepdims=True)
    o_ref[0] = x * lax.rsqrt(ms + EPS) * g_ref[...]


def _final_call(xs, g, n_lat):
    batch = xs.shape[0]
    tile = pl.BlockSpec((1, ROW_TILE, D_MODEL), lambda b, i: (b, i, 0))
    return pl.pallas_call(
        _final_kernel,
        grid=(batch, n_lat // ROW_TILE),
        in_specs=[tile, pl.BlockSpec((1, D_MODEL), lambda b, i: (0, 0))],
        out_specs=tile,
        out_shape=jax.ShapeDtypeStruct((batch, n_lat, D_MODEL), F32),
        compiler_params=_params("arbitrary", "arbitrary"),
        name="final_norm",
    )(xs, g)


def _rope_tables(n_lat, n_ctx):
    rows = n_lat // GRID_W
    half = HEAD_DIM // 2
    row = jnp.repeat(jnp.arange(rows), GRID_W).astype(F32)
    col = jnp.tile(jnp.arange(GRID_W), rows).astype(F32)
    inv = ROPE_BASE ** (-jnp.arange(0, half, 2, dtype=F32) / half)
    ang = jnp.concatenate([row[:, None] * inv, col[:, None] * inv], axis=-1)
    cos, sin = jnp.cos(ang), jnp.sin(ang)
    reps = LANES // HEAD_DIM
    cos_t = jnp.tile(jnp.concatenate([cos, cos], axis=-1), (1, reps))
    sin_t = jnp.tile(jnp.concatenate([-sin, sin], axis=-1), (1, reps))
    cos_t = jnp.concatenate([cos_t, jnp.ones((n_ctx, LANES), F32)], axis=0)
    sin_t = jnp.concatenate([sin_t, jnp.zeros((n_ctx, LANES), F32)], axis=0)
    return cos_t, sin_t


def _pool_bands():
    t = jnp.arange(ROW_TILE)[:, None]
    j = jnp.arange(ROW_TILE + 2 * HALO)[None, :]
    d = j - HALO - t
    return jnp.stack([((d >= -(w // 2)) & (d < w - w // 2)).astype(BF16) for w in POOL_WINDOWS])


def kernel(x, c, ctx, c_ctx, w_ada, b_ada, g_mix, g_mlp, w_in, w_pool, s_pool,
           lam_q1, lam_k1, lam_q2, lam_k2, g_subln, w_out, w_mlp1, w_mlp2, g_final):
    batch, n_lat, d_model = x.shape
    n_ctx = ctx.shape[1]
    depth = w_ada.shape[0]
    assert d_model == D_MODEL and batch < ADA_ROWS
    assert n_lat % KV_CHUNK == 0 and n_lat % GRID_W == 0 and n_ctx % ROW_TILE == 0
    lat_tiles = n_lat // ROW_TILE

    xs = jnp.concatenate([x, ctx], axis=1)
    cvec = jnp.zeros((ADA_ROWS, D_MODEL), F32).at[:batch].set(c).at[batch].set(c_ctx)
    lam_init = [0.8 - 0.6 * math.exp(-0.3 * l) for l in range(depth)]
    linit = jnp.broadcast_to(jnp.asarray(lam_init, F32)[:, None, None], (depth, 1, LANES))
    vec3 = lambda a: a.reshape(depth, 1, a.shape[-1])
    mods, lam = _ada_call(cvec, w_ada, vec3(b_ada), vec3(lam_q1), vec3(lam_k1),
                          vec3(lam_q2), vec3(lam_k2), linit)
    mods = mods.reshape(depth, ADA_ROWS, N_ADA, D_MODEL)
    scal = jnp.stack([lam[:, 0, 0], 1.0 - jnp.asarray(lam_init, F32)], axis=-1)

    cos, sin = _rope_tables(n_lat, n_ctx)
    band = _pool_bands()
    w_in_b = w_in.astype(BF16)
    w_pool_b = w_pool.astype(BF16)
    w_out_b = w_out.astype(BF16)
    w1_b = w_mlp1.astype(BF16)
    w2_b = w_mlp2.astype(BF16)

    for l in range(depth):
        u, q, k, v = _inproj_call(xs, mods[l], g_mix[l][None], w_in_b[l], cos, sin, lat_tiles)
        att = _attn_call(scal[l], q, k, v, g_subln[l][None], n_lat)
        xs = _mix_call(xs, u, att, mods[l], band, w_pool_b[l], s_pool[l][None], w_out_b[l],
                       n_lat, lat_tiles)
        xs = _mlp_call(xs, mods[l], g_mlp[l][None], w1_b[l], w2_b[l], lat_tiles)
    return _final_call(xs, g_final[None], n_lat)
```

```python
import functools
import math

import jax
import jax.numpy as jnp
from jax import lax
from jax.experimental import pallas as pl
from jax.experimental.pallas import tpu as pltpu

D_MODEL = 1024
GRID_W = 64
POOL_WIDTH = 512
POOL_WINDOWS = (2, 4, 8, 16)
POOL_GC = 128
N_HEADS = 4
HEAD_DIM = 64
V_DIM = 128
QK_WIDTH = 512
ATTN_WIDTH = 512
IN_WIDTH = 2048
ROPE_BASE = 10000.0
D_FF = 4096
N_ADA = 6
EPS = 1e-6

LANES = 128
ADA_ROWS = 8
ROW_TILE = 256
HALO = 16
KV_CHUNK = 512
FF_CHUNK = 1024
ADA_COLS = 1024
VMEM_LIMIT = 56 * 1024 * 1024

F32 = jnp.float32
BF16 = jnp.bfloat16


def _params(*sem):
    return pltpu.CompilerParams(dimension_semantics=sem, vmem_limit_bytes=VMEM_LIMIT)


def _ada_kernel(c_ref, w_ref, b_ref, lq1_ref, lk1_ref, lq2_ref, lk2_ref, linit_ref,
                mod_ref, lam_ref):
    c = c_ref[...]
    s = c / (1.0 + jnp.exp(-c))
    mod_ref[0] = jnp.dot(s, w_ref[0], preferred_element_type=F32,
                         precision=lax.Precision.HIGHEST) + b_ref[0]
    d1 = jnp.sum(lq1_ref[0] * lk1_ref[0], axis=-1, keepdims=True)
    d2 = jnp.sum(lq2_ref[0] * lk2_ref[0], axis=-1, keepdims=True)
    lam = jnp.exp(d1) - jnp.exp(d2) + linit_ref[0]
    lam_ref[0] = jnp.broadcast_to(lam, (ADA_ROWS, LANES))


def _ada_call(cvec, w_ada, b_ada, lq1, lk1, lq2, lk2, linit):
    depth = w_ada.shape[0]
    width = w_ada.shape[2]
    vec = lambda: pl.BlockSpec((1, 1, HEAD_DIM), lambda l, j: (l, 0, 0))
    return pl.pallas_call(
        _ada_kernel,
        grid=(depth, width // ADA_COLS),
        in_specs=[
            pl.BlockSpec((ADA_ROWS, D_MODEL), lambda l, j: (0, 0)),
            pl.BlockSpec((1, D_MODEL, ADA_COLS), lambda l, j: (l, 0, j)),
            pl.BlockSpec((1, 1, ADA_COLS), lambda l, j: (l, 0, j)),
            vec(), vec(), vec(), vec(),
            pl.BlockSpec((1, 1, LANES), lambda l, j: (l, 0, 0)),
        ],
        out_specs=[
            pl.BlockSpec((1, ADA_ROWS, ADA_COLS), lambda l, j: (l, 0, j)),
            pl.BlockSpec((1, ADA_ROWS, LANES), lambda l, j: (l, 0, 0)),
        ],
        out_shape=[
            jax.ShapeDtypeStruct((depth, ADA_ROWS, width), F32),
            jax.ShapeDtypeStruct((depth, ADA_ROWS, LANES), F32),
        ],
        compiler_params=_params("arbitrary", "arbitrary"),
        name="ada",
    )(cvec, w_ada, b_ada, lq1, lk1, lq2, lk2, linit)


def _mod_spec(batch, lat_tiles):
    return pl.BlockSpec((1, N_ADA, D_MODEL),
                        lambda b, i: (jnp.where(i >= lat_tiles, batch, b), 0, 0))


def _norm_mod(x, g, shift, scale):
    ms = jnp.mean(x * x, axis=-1, keepdims=True)
    return (x * lax.rsqrt(ms + EPS) * g) * (1.0 + scale) + shift


def _inproj_kernel(x_ref, mod_ref, g_ref, w_ref, cos_ref, sin_ref,
                   u_ref, q_ref, k_ref, v_ref):
    h = _norm_mod(x_ref[0], g_ref[...], mod_ref[0, 0:1, :], mod_ref[0, 1:2, :])
    z = jnp.dot(h.astype(BF16), w_ref[...], preferred_element_type=F32)
    u_ref[0] = z[:, :POOL_WIDTH]
    cos = cos_ref[...]
    sin = sin_ref[...]
    lane = lax.broadcasted_iota(jnp.int32, cos.shape, 1)
    first_half = (lane & (HEAD_DIM // 2)) == 0
    q_scale = HEAD_DIM ** -0.5 * math.log2(math.e)

    def rope(t):
        partner = jnp.where(first_half,
                            pltpu.roll(t, LANES - HEAD_DIM // 2, 1),
                            pltpu.roll(t, HEAD_DIM // 2, 1))
        return t * cos + partner * sin

    for c in range(QK_WIDTH // LANES):
        lo = POOL_WIDTH + c * LANES
        q_ref[0, :, c * LANES:(c + 1) * LANES] = (rope(z[:, lo:lo + LANES]) * q_scale).astype(BF16)
        lo += QK_WIDTH
        k_ref[0, :, c * LANES:(c + 1) * LANES] = rope(z[:, lo:lo + LANES]).astype(BF16)
    ones = jnp.ones((z.shape[0], V_DIM), BF16)
    for hd in range(N_HEADS):
        lo = POOL_WIDTH + 2 * QK_WIDTH + hd * V_DIM
        v_ref[0, :, 2 * hd * V_DIM:(2 * hd + 1) * V_DIM] = z[:, lo:lo + V_DIM].astype(BF16)
        v_ref[0, :, (2 * hd + 1) * V_DIM:(2 * hd + 2) * V_DIM] = ones


def _inproj_call(xs, mods, g, w_in, cos, sin, lat_tiles):
    batch, rows, _ = xs.shape
    tile = lambda width: pl.BlockSpec((1, ROW_TILE, width), lambda b, i: (b, i, 0))
    table = pl.BlockSpec((ROW_TILE, LANES), lambda b, i: (i, 0))
    out = lambda dt: jax.ShapeDtypeStruct((batch, rows, POOL_WIDTH), dt)
    return pl.pallas_call(
        _inproj_kernel,
        grid=(batch, rows // ROW_TILE),
        in_specs=[
            tile(D_MODEL),
            _mod_spec(batch, lat_tiles),
            pl.BlockSpec((1, D_MODEL), lambda b, i: (0, 0)),
            pl.BlockSpec((D_MODEL, IN_WIDTH), lambda b, i: (0, 0)),
            table, table,
        ],
        out_specs=[tile(POOL_WIDTH)] * 3 + [tile(2 * ATTN_WIDTH)],
        out_shape=[out(F32), out(BF16), out(BF16),
                   jax.ShapeDtypeStruct((batch, rows, 2 * ATTN_WIDTH), BF16)],
        compiler_params=_params("arbitrary", "arbitrary"),
        name="inproj",
    )(xs, mods, g, w_in, cos, sin)


def _attn_kernel(scal_ref, q_ref, k_ref, v_ref, g_ref, o_ref, *, n_lat, n_ctx):
    i = pl.program_id(2)
    tq = q_ref.shape[1]
    q = q_ref[0]
    lane = lax.broadcasted_iota(jnp.int32, q.shape, 1)
    zero = jnp.zeros_like(q)
    qq = jnp.concatenate([jnp.where(lane < HEAD_DIM, q, zero),
                          jnp.where(lane >= HEAD_DIM, q, zero)], axis=0)

    def step(carry, start, size):
        m, acc = carry
        kc = k_ref[0, start:start + size, :]
        vc = v_ref[0, start:start + size, :]
        s = lax.dot_general(qq, kc, (((1,), (1,)), ((), ())), preferred_element_type=F32)
        m_new = jnp.maximum(m, jnp.max(s, axis=-1, keepdims=True))
        p = jnp.exp2(s - m_new).astype(BF16)
        pv = jnp.dot(p, vc, preferred_element_type=F32)
        if acc is None:
            return m_new, pv
        return m_new, jnp.exp2(m - m_new) * acc + pv

    def finish(acc):
        o = acc[:, :V_DIM] / acc[:, V_DIM:]
        o = o[:tq] - scal_ref[0] * o[tq:]
        ms = jnp.mean(o * o, axis=-1, keepdims=True)
        o_ref[0] = (o * lax.rsqrt(ms + EPS) * g_ref[...] * scal_ref[1]).astype(o_ref.dtype)

    init = (jnp.full((2 * tq, 1), -jnp.inf, F32), None)

    @pl.when(i * tq < n_lat)
    def _():
        carry = init
        for c in range(n_lat // KV_CHUNK):
            carry = step(carry, c * KV_CHUNK, KV_CHUNK)
        finish(step(carry, n_lat, n_ctx)[1])

    @pl.when(i * tq >= n_lat)
    def _():
        finish(step(init, n_lat, n_ctx)[1])


def _attn_call(scal, q, k, v, g, n_lat):
    batch, rows, _ = q.shape
    qo = pl.BlockSpec((1, ROW_TILE, V_DIM), lambda b, h, i: (b, i, h))
    keys = pl.BlockSpec((1, rows, V_DIM), lambda b, h, i: (b, 0, h))
    vals = pl.BlockSpec((1, rows, 2 * V_DIM), lambda b, h, i: (b, 0, h))
    return pl.pallas_call(
        functools.partial(_attn_kernel, n_lat=n_lat, n_ctx=rows - n_lat),
        grid=(batch, N_HEADS, rows // ROW_TILE),
        in_specs=[
            pl.BlockSpec(memory_space=pltpu.SMEM),
            qo, keys, vals,
            pl.BlockSpec((1, V_DIM), lambda b, h, i: (0, 0)),
        ],
        out_specs=qo,
        out_shape=jax.ShapeDtypeStruct((batch, rows, ATTN_WIDTH), BF16),
        compiler_params=_params("arbitrary", "arbitrary", "arbitrary"),
        name="attn",
    )(scal, q, k, v, g)


def _mix_kernel(x_ref, u_ref, up_ref, un_ref, att_ref, mod_ref, band_ref, wp_ref, sp_ref,
                wo_ref, o_ref, *, n_lat, n_all):
    r0 = pl.program_id(1) * ROW_TILE
    in_ctx = r0 >= n_lat
    seg_lo = jnp.where(in_ctx, n_lat, 0)
    seg_hi = jnp.where(in_ctx, n_all, n_lat)
    cur = u_ref[0]
    ext = jnp.concatenate([up_ref[0], cur, un_ref[0]], axis=0)
    srow = r0 - HALO + lax.broadcasted_iota(jnp.int32, (ROW_TILE + 2 * HALO, 1), 0)
    ext = jnp.where((srow >= seg_lo) & (srow < seg_hi), ext, 0.0).astype(BF16)
    trow = r0 + lax.broadcasted_iota(jnp.int32, (ROW_TILE, 1), 0)
    mix = jnp.dot(att_ref[0], wo_ref[POOL_WIDTH:, :], preferred_element_type=F32)
    for gi, w in enumerate(POOL_WINDOWS):
        cols = slice(gi * POOL_GC, (gi + 1) * POOL_GC)
        tot = jnp.dot(band_ref[gi], ext[:, cols], preferred_element_type=F32)
        cnt = (jnp.minimum(trow + (w - w // 2), seg_hi) - jnp.maximum(trow - w // 2, seg_lo))
        y = tot / cnt.astype(F32) - cur[:, cols]
        p = jnp.dot(y.astype(BF16), wp_ref[gi], preferred_element_type=F32) * sp_ref[:, cols]
        mix += jnp.dot(p.astype(BF16), wo_ref[cols, :], preferred_element_type=F32)
    o_ref[0] = x_ref[0] + mod_ref[0, 2:3, :] * mix


def _mix_call(xs, u, att, mods, band, w_pool, s_pool, w_out, n_lat, lat_tiles):
    batch, rows, _ = xs.shape
    halo_per_tile = ROW_TILE // HALO
    last_halo = rows // HALO - 1
    tile = lambda width: pl.BlockSpec((1, ROW_TILE, width), lambda b, i: (b, i, 0))
    return pl.pallas_call(
        functools.partial(_mix_kernel, n_lat=n_lat, n_all=rows),
        grid=(batch, rows // ROW_TILE),
        in_specs=[
            tile(D_MODEL),
            tile(POOL_WIDTH),
            pl.BlockSpec((1, HALO, POOL_WIDTH),
                         lambda b, i: (b, jnp.maximum(i * halo_per_tile - 1, 0), 0)),
            pl.BlockSpec((1, HALO, POOL_WIDTH),
                         lambda b, i: (b, jnp.minimum((i + 1) * halo_per_tile, last_halo), 0)),
            tile(ATTN_WIDTH),
            _mod_spec(batch, lat_tiles),
            pl.BlockSpec(band.shape, lambda b, i: (0, 0, 0)),
            pl.BlockSpec(w_pool.shape, lambda b, i: (0, 0, 0)),
            pl.BlockSpec((1, POOL_WIDTH), lambda b, i: (0, 0)),
            pl.BlockSpec(w_out.shape, lambda b, i: (0, 0)),
        ],
        out_specs=tile(D_MODEL),
        out_shape=jax.ShapeDtypeStruct(xs.shape, F32),
        compiler_params=_params("arbitrary", "arbitrary"),
        name="mix",
    )(xs, u, u, u, att, mods, band, w_pool, s_pool, w_out)


def _mlp_kernel(x_ref, mod_ref, g_ref, w1_ref, w2_ref, o_ref):
    x = x_ref[0]
    h = _norm_mod(x, g_ref[...], mod_ref[0, 3:4, :], mod_ref[0, 4:5, :]).astype(BF16)
    acc = jnp.zeros(x.shape, F32)
    for c in range(D_FF // FF_CHUNK):
        cols = slice(c * FF_CHUNK, (c + 1) * FF_CHUNK)
        a = jnp.dot(h, w1_ref[:, cols], preferred_element_type=F32)
        a = jnp.square(jnp.maximum(a, 0.0)).astype(BF16)
        acc += jnp.dot(a, w2_ref[cols, :], preferred_element_type=F32)
    o_ref[0] = x + mod_ref[0, 5:6, :] * acc


def _mlp_call(xs, mods, g, w1, w2, lat_tiles):
    batch, rows, _ = xs.shape
    tile = pl.BlockSpec((1, ROW_TILE, D_MODEL), lambda b, i: (b, i, 0))
    return pl.pallas_call(
        _mlp_kernel,
        grid=(batch, rows // ROW_TILE),
        in_specs=[
            tile,
            _mod_spec(batch, lat_tiles),
            pl.BlockSpec((1, D_MODEL), lambda b, i: (0, 0)),
            pl.BlockSpec(w1.shape, lambda b, i: (0, 0)),
            pl.BlockSpec(w2.shape, lambda b, i: (0, 0)),
        ],
        out_specs=tile,
        out_shape=jax.ShapeDtypeStruct(xs.shape, F32),
        compiler_params=_params("arbitrary", "arbitrary"),
        name="mlp",
    )(xs, mods, g, w1, w2)


def _final_kernel(x_ref, g_ref, o_ref):
    x = x_ref[0]
    ms = jnp.mean(x * x, axis=-1, keepdims=True)
    o_ref[0] = x * lax.rsqrt(ms + EPS) * g_ref[...]


def _final_call(xs, g, n_lat):
    batch = xs.shape[0]
    tile = pl.BlockSpec((1, ROW_TILE, D_MODEL), lambda b, i: (b, i, 0))
    return pl.pallas_call(
        _final_kernel,
        grid=(batch, n_lat // ROW_TILE),
        in_specs=[tile, pl.BlockSpec((1, D_MODEL), lambda b, i: (0, 0))],
        out_specs=tile,
        out_shape=jax.ShapeDtypeStruct((batch, n_lat, D_MODEL), F32),
        compiler_params=_params("arbitrary", "arbitrary"),
        name="final_norm",
    )(xs, g)


def _rope_tables(n_lat, n_ctx):
    rows = n_lat // GRID_W
    half = HEAD_DIM // 2
    row = jnp.repeat(jnp.arange(rows), GRID_W).astype(F32)
    col = jnp.tile(jnp.arange(GRID_W), rows).astype(F32)
    inv = ROPE_BASE ** (-jnp.arange(0, half, 2, dtype=F32) / half)
    ang = jnp.concatenate([row[:, None] * inv, col[:, None] * inv], axis=-1)
    cos, sin = jnp.cos(ang), jnp.sin(ang)
    reps = LANES // HEAD_DIM
    cos_t = jnp.tile(jnp.concatenate([cos, cos], axis=-1), (1, reps))
    sin_t = jnp.tile(jnp.concatenate([-sin, sin], axis=-1), (1, reps))
    cos_t = jnp.concatenate([cos_t, jnp.ones((n_ctx, LANES), F32)], axis=0)
    sin_t = jnp.concatenate([sin_t, jnp.zeros((n_ctx, LANES), F32)], axis=0)
    return cos_t, sin_t


def _pool_bands():
    t = jnp.arange(ROW_TILE)[:, None]
    j = jnp.arange(ROW_TILE + 2 * HALO)[None, :]
    d = j - HALO - t
    return jnp.stack([((d >= -(w // 2)) & (d < w - w // 2)).astype(BF16) for w in POOL_WINDOWS])


def kernel(x, c, ctx, c_ctx, w_ada, b_ada, g_mix, g_mlp, w_in, w_pool, s_pool,
           lam_q1, lam_k1, lam_q2, lam_k2, g_subln, w_out, w_mlp1, w_mlp2, g_final):
    batch, n_lat, d_model = x.shape
    n_ctx = ctx.shape[1]
    depth = w_ada.shape[0]
    assert d_model == D_MODEL and batch < ADA_ROWS
    assert n_lat % KV_CHUNK == 0 and n_lat % GRID_W == 0 and n_ctx % ROW_TILE == 0
    lat_tiles = n_lat // ROW_TILE

    xs = jnp.concatenate([x, ctx], axis=1)
    cvec = jnp.zeros((ADA_ROWS, D_MODEL), F32).at[:batch].set(c).at[batch].set(c_ctx)
    lam_init = [0.8 - 0.6 * math.exp(-0.3 * l) for l in range(depth)]
    linit = jnp.broadcast_to(jnp.asarray(lam_init, F32)[:, None, None], (depth, 1, LANES))
    vec3 = lambda a: a.reshape(depth, 1, a.shape[-1])
    mods, lam = _ada_call(cvec, w_ada, vec3(b_ada), vec3(lam_q1), vec3(lam_k1),
                          vec3(lam_q2), vec3(lam_k2), linit)
    mods = mods.reshape(depth, ADA_ROWS, N_ADA, D_MODEL)
    scal = jnp.stack([lam[:, 0, 0], 1.0 - jnp.asarray(lam_init, F32)], axis=-1)

    cos, sin = _rope_tables(n_lat, n_ctx)
    band = _pool_bands()
    w_in_b = w_in.astype(BF16)
    w_pool_b = w_pool.astype(BF16)
    w_out_b = w_out.astype(BF16)
    w1_b = w_mlp1.astype(BF16)
    w2_b = w_mlp2.astype(BF16)

    for l in range(depth):
        u, q, k, v = _inproj_call(xs, mods[l], g_mix[l][None], w_in_b[l], cos, sin, lat_tiles)
        att = _attn_call(scal[l], q, k, v, g_subln[l][None], n_lat)
        xs = _mix_call(xs, u, att, mods[l], band, w_pool_b[l], s_pool[l][None], w_out_b[l],
                       n_lat, lat_tiles)
        xs = _mlp_call(xs, mods[l], g_mlp[l][None], w1_b[l], w2_b[l], lat_tiles)
    return _final_call(xs, g_final[None], n_lat)
```

```python
import functools
import math

import jax
import jax.numpy as jnp
from jax import lax
from jax.experimental import pallas as pl
from jax.experimental.pallas import tpu as pltpu

D_MODEL = 1024
GRID_W = 64
POOL_WIDTH = 512
POOL_WINDOWS = (2, 4, 8, 16)
POOL_GC = 128
N_HEADS = 4
HEAD_DIM = 64
V_DIM = 128
QK_WIDTH = 512
ATTN_WIDTH = 512
IN_WIDTH = 2048
ROPE_BASE = 10000.0
D_FF = 4096
N_ADA = 6
EPS = 1e-6

LANES = 128
ADA_ROWS = 8
ROW_TILE = 1024
MLP_TILE = 512
Q_TILE = 1024
POOL_SUB = 128
HALO = 16
KV_CHUNK = 256
FF_CHUNK = 1024
ADA_COLS = 1024
VMEM_LIMIT = 56 * 1024 * 1024

F32 = jnp.float32
BF16 = jnp.bfloat16


def _params(*sem):
    return pltpu.CompilerParams(dimension_semantics=sem, vmem_limit_bytes=VMEM_LIMIT)


class _Rows:
    def __init__(self, batch, n_lat, n_ctx):
        self.batch, self.n_lat, self.n_ctx = batch, n_lat, n_ctx
        self.lat = batch * n_lat
        self.ctx = batch * n_ctx
        self.all = self.lat + self.ctx

    def mod_spec(self, layer, tile):
        lat_tiles, per_batch = self.lat // tile, self.n_lat // tile
        return pl.BlockSpec(
            (None, None, N_ADA, D_MODEL),
            lambda t: (layer, jnp.where(t < lat_tiles, t // per_batch, self.batch), 0, 0))

    def stream_specs(self, tile, width, split):
        if not split:
            return [pl.BlockSpec((tile, width), lambda t: (t, 0))]
        lat_tiles = self.lat // tile
        return [pl.BlockSpec((tile, width), lambda t: (jnp.minimum(t, lat_tiles - 1), 0)),
                pl.BlockSpec((tile, width), lambda t: (jnp.maximum(t - lat_tiles, 0), 0))]


def _stream_rows(refs, lat_tiles):
    if len(refs) == 1:
        return refs[0][...]
    return jnp.where(pl.program_id(0) >= lat_tiles, refs[1][...], refs[0][...])


def _layer_spec(layer, *shape):
    return pl.BlockSpec((None,) + shape, lambda t: (layer,) + (0,) * len(shape))


def _norm_mod(x, g, shift, scale):
    ms = jnp.mean(x * x, axis=-1, keepdims=True)
    return (x * lax.rsqrt(ms + EPS) * g) * (1.0 + scale) + shift


def _ada_kernel(c_ref, w_ref, b_ref, lq1_ref, lk1_ref, lq2_ref, lk2_ref, linit_ref,
                mod_ref, lam_ref):
    c = c_ref[...]
    s = c / (1.0 + jnp.exp(-c))
    mod_ref[0] = jnp.dot(s, w_ref[0], preferred_element_type=F32,
                         precision=lax.Precision.HIGHEST) + b_ref[0]
    d1 = jnp.sum(lq1_ref[0] * lk1_ref[0], axis=-1, keepdims=True)
    d2 = jnp.sum(lq2_ref[0] * lk2_ref[0], axis=-1, keepdims=True)
    lam = jnp.exp(d1) - jnp.exp(d2) + linit_ref[0]
    lam_ref[0] = jnp.broadcast_to(lam, (ADA_ROWS, LANES))


def _ada_call(cvec, w_ada, b_ada, lq1, lk1, lq2, lk2, linit):
    depth = w_ada.shape[0]
    width = w_ada.shape[2]
    vec = lambda: pl.BlockSpec((1, 1, HEAD_DIM), lambda l, j: (l, 0, 0))
    return pl.pallas_call(
        _ada_kernel,
        grid=(depth, width // ADA_COLS),
        in_specs=[
            pl.BlockSpec((ADA_ROWS, D_MODEL), lambda l, j: (0, 0)),
            pl.BlockSpec((1, D_MODEL, ADA_COLS), lambda l, j: (l, 0, j)),
            pl.BlockSpec((1, 1, ADA_COLS), lambda l, j: (l, 0, j)),
            vec(), vec(), vec(), vec(),
            pl.BlockSpec((1, 1, LANES), lambda l, j: (l, 0, 0)),
        ],
        out_specs=[
            pl.BlockSpec((1, ADA_ROWS, ADA_COLS), lambda l, j: (l, 0, j)),
            pl.BlockSpec((1, ADA_ROWS, LANES), lambda l, j: (l, 0, 0)),
        ],
        out_shape=[
            jax.ShapeDtypeStruct((depth, ADA_ROWS, width), F32),
            jax.ShapeDtypeStruct((depth, ADA_ROWS, LANES), F32),
        ],
        compiler_params=_params("arbitrary", "arbitrary"),
        name="ada",
    )(cvec, w_ada, b_ada, lq1, lk1, lq2, lk2, linit)


def _inproj_kernel(*refs, lat_tiles):
    *x_refs, mod_ref, g_ref, w_ref, cos_ref, sin_ref, u_ref, q_ref, k_ref, v_ref = refs
    x = _stream_rows(x_refs, lat_tiles)
    h = _norm_mod(x, g_ref[...], mod_ref[0:1, :], mod_ref[1:2, :]).astype(BF16)
    cos = cos_ref[...]
    sin = sin_ref[...]
    lane = lax.broadcasted_iota(jnp.int32, cos.shape, 1)
    first_half = (lane & (HEAD_DIM // 2)) == 0
    q_scale = HEAD_DIM ** -0.5 * math.log2(math.e)

    def rope(t):
        partner = jnp.where(first_half,
                            pltpu.roll(t, LANES - HEAD_DIM // 2, 1),
                            pltpu.roll(t, HEAD_DIM // 2, 1))
        return t * cos + partner * sin

    def proj(lo, width):
        return jnp.dot(h, w_ref[:, lo:lo + width], preferred_element_type=F32)

    u_ref[...] = proj(0, POOL_WIDTH)
    z = proj(POOL_WIDTH, QK_WIDTH)
    for c in range(QK_WIDTH // LANES):
        cols = slice(c * LANES, (c + 1) * LANES)
        q_ref[:, cols] = (rope(z[:, cols]) * q_scale).astype(BF16)
    z = proj(POOL_WIDTH + QK_WIDTH, QK_WIDTH)
    for c in range(QK_WIDTH // LANES):
        cols = slice(c * LANES, (c + 1) * LANES)
        k_ref[:, cols] = rope(z[:, cols]).astype(BF16)
    z = proj(POOL_WIDTH + 2 * QK_WIDTH, ATTN_WIDTH)
    ones = jnp.ones((z.shape[0], V_DIM), BF16)
    for hd in range(N_HEADS):
        v_ref[:, 2 * hd * V_DIM:(2 * hd + 1) * V_DIM] = z[:, hd * V_DIM:(hd + 1) * V_DIM].astype(BF16)
        v_ref[:, (2 * hd + 1) * V_DIM:(2 * hd + 2) * V_DIM] = ones


def _inproj_call(streams, rows, layer, mods, g, w_in, cos, sin):
    tile = ROW_TILE
    lat_tiles, per_batch = rows.lat // tile, rows.n_lat // tile
    split = len(streams) == 2
    table = pl.BlockSpec((tile, LANES),
                         lambda t: (jnp.where(t < lat_tiles, t % per_batch, per_batch), 0))
    out = lambda width: pl.BlockSpec((tile, width), lambda t: (t, 0))
    shape = lambda width, dt: jax.ShapeDtypeStruct((rows.all, width), dt)
    return pl.pallas_call(
        functools.partial(_inproj_kernel, lat_tiles=lat_tiles),
        grid=(rows.all // tile,),
        in_specs=rows.stream_specs(tile, D_MODEL, split) + [
            rows.mod_spec(layer, tile),
            _layer_spec(layer, 1, D_MODEL),
            _layer_spec(layer, D_MODEL, IN_WIDTH),
            table, table,
        ],
        out_specs=[out(POOL_WIDTH), out(QK_WIDTH), out(QK_WIDTH), out(2 * ATTN_WIDTH)],
        out_shape=[shape(POOL_WIDTH, F32), shape(QK_WIDTH, BF16), shape(QK_WIDTH, BF16),
                   shape(2 * ATTN_WIDTH, BF16)],
        compiler_params=_params("arbitrary"),
        name="inproj",
    )(*streams, mods, g, w_in, cos, sin)


def _attn_kernel(scal_ref, q_ref, *refs, layer, latent):
    if latent:
        kl_ref, vl_ref, kc_ref, vc_ref, g_ref, o_ref = refs
    else:
        kc_ref, vc_ref, g_ref, _, o_ref = refs
    tq = q_ref.shape[0]
    q = q_ref[...]
    lane = lax.broadcasted_iota(jnp.int32, q.shape, 1)
    zero = jnp.zeros_like(q)
    qq = jnp.concatenate([jnp.where(lane < HEAD_DIM, q, zero),
                          jnp.where(lane >= HEAD_DIM, q, zero)], axis=0)

    def step(carry, kc, vc):
        m, acc = carry
        s = lax.dot_general(qq, kc, (((1,), (1,)), ((), ())), preferred_element_type=F32)
        m_new = jnp.maximum(m, jnp.max(s, axis=-1, keepdims=True))
        p = jnp.exp2(s - m_new).astype(BF16)
        pv = jnp.dot(p, vc, preferred_element_type=F32)
        if acc is None:
            return m_new, pv
        return m_new, jnp.exp2(m - m_new) * acc + pv

    carry = (jnp.full((2 * tq, 1), -jnp.inf, F32), None)
    if latent:
        for c in range(kl_ref.shape[0] // KV_CHUNK):
            rows = slice(c * KV_CHUNK, (c + 1) * KV_CHUNK)
            carry = step(carry, kl_ref[rows, :], vl_ref[rows, :])
    acc = step(carry, kc_ref[...], vc_ref[...])[1]
    o = acc[:, :V_DIM] / acc[:, V_DIM:]
    o = o[:tq] - scal_ref[layer, 0] * o[tq:]
    ms = jnp.mean(o * o, axis=-1, keepdims=True)
    o_ref[...] = (o * lax.rsqrt(ms + EPS) * g_ref[...] * scal_ref[layer, 1]).astype(o_ref.dtype)


def _attn_call(scal, q, k, v, g, rows, layer, with_ctx):
    ctx_blocks = rows.lat // rows.n_ctx
    lat_tiles = rows.n_lat // Q_TILE
    smem = pl.BlockSpec(memory_space=pltpu.SMEM)
    lat = lambda width: pl.BlockSpec((rows.n_lat, width), lambda b, h, *i: (b, h))
    ctx = lambda width: pl.BlockSpec((rows.n_ctx, width), lambda b, h, *i: (ctx_blocks + b, h))
    gain = pl.BlockSpec((None, 1, V_DIM), lambda b, h, *i: (layer, 0, 0))
    out_shape = jax.ShapeDtypeStruct((rows.all if with_ctx else rows.lat, ATTN_WIDTH), BF16)
    q_lat = pl.BlockSpec((Q_TILE, V_DIM), lambda b, h, i: (b * lat_tiles + i, h))
    att = pl.pallas_call(
        functools.partial(_attn_kernel, layer=layer, latent=True),
        grid=(rows.batch, N_HEADS, lat_tiles),
        in_specs=[smem, q_lat, lat(V_DIM), lat(2 * V_DIM), ctx(V_DIM), ctx(2 * V_DIM), gain],
        out_specs=q_lat,
        out_shape=out_shape,
        compiler_params=_params("arbitrary", "arbitrary", "arbitrary"),
        name="attn",
    )(scal, q, k, v, k, v, g)
    if not with_ctx:
        return att
    return pl.pallas_call(
        functools.partial(_attn_kernel, layer=layer, latent=False),
        grid=(rows.batch, N_HEADS),
        in_specs=[smem, ctx(V_DIM), ctx(V_DIM), ctx(2 * V_DIM), gain,
                  pl.BlockSpec(memory_space=pl.ANY)],
        out_specs=ctx(V_DIM),
        out_shape=out_shape,
        input_output_aliases={5: 0},
        compiler_params=_params("arbitrary", "arbitrary"),
        name="attn_ctx",
    )(scal, q, k, v, g, att)


def _mix_kernel(*refs, rows, tile):
    (*x_refs, u_ref, up_ref, un_ref, att_ref, mod_ref, band_ref, wp_ref, sp_ref, wo_ref,
     o_ref, pool_ref) = refs
    lat_tiles = rows.lat // tile
    r0 = pl.program_id(0) * tile
    in_ctx = r0 >= rows.lat
    seg_len = jnp.where(in_ctx, rows.n_ctx, rows.n_lat)
    seg_base = jnp.where(in_ctx, rows.lat, 0)
    for j in range(tile // POOL_SUB):
        start = r0 + j * POOL_SUB
        seg_lo = seg_base + lax.div(start - seg_base, seg_len) * seg_len
        seg_hi = seg_lo + seg_len
        lo, hi = j * POOL_SUB - HALO, (j + 1) * POOL_SUB + HALO
        parts = [up_ref[...]] if lo < 0 else []
        parts.append(u_ref[max(lo, 0):min(hi, tile), :])
        if hi > tile:
            parts.append(un_ref[...])
        ext = jnp.concatenate(parts, axis=0) if len(parts) > 1 else parts[0]
        srow = start - HALO + lax.broadcasted_iota(jnp.int32, (POOL_SUB + 2 * HALO, 1), 0)
        ext = jnp.where((srow >= seg_lo) & (srow < seg_hi), ext, 0.0).astype(BF16)
        trow = start + lax.broadcasted_iota(jnp.int32, (POOL_SUB, 1), 0)
        sub = slice(j * POOL_SUB, (j + 1) * POOL_SUB)
        for gi, w in enumerate(POOL_WINDOWS):
            cols = slice(gi * POOL_GC, (gi + 1) * POOL_GC)
            tot = jnp.dot(band_ref[gi], ext[:, cols], preferred_element_type=F32)
            cnt = jnp.minimum(trow + (w - w // 2), seg_hi) - jnp.maximum(trow - w // 2, seg_lo)
            y = tot / cnt.astype(F32) - u_ref[sub, cols]
            p = jnp.dot(y.astype(BF16), wp_ref[gi], preferred_element_type=F32) * sp_ref[:, cols]
            pool_ref[sub, cols] = p.astype(BF16)
    mix = jnp.dot(pool_ref[...], wo_ref[:POOL_WIDTH, :], preferred_element_type=F32)
    mix += jnp.dot(att_ref[...], wo_ref[POOL_WIDTH:, :], preferred_element_type=F32)
    o_ref[...] = _stream_rows(x_refs, lat_tiles) + mod_ref[2:3, :] * mix


def _mix_call(streams, u, att, rows, layer, mods, band, w_pool, s_pool, w_out, with_ctx):
    tile = ROW_TILE
    n_rows = rows.all if with_ctx else rows.lat
    split = len(streams) == 2
    halo_per_tile = tile // HALO
    last_halo = rows.all // HALO - 1
    full = lambda width: pl.BlockSpec((tile, width), lambda t: (t, 0))
    return pl.pallas_call(
        functools.partial(_mix_kernel, rows=rows, tile=tile),
        grid=(n_rows // tile,),
        in_specs=rows.stream_specs(tile, D_MODEL, split) + [
            full(POOL_WIDTH),
            pl.BlockSpec((HALO, POOL_WIDTH), lambda t: (jnp.maximum(t * halo_per_tile - 1, 0), 0)),
            pl.BlockSpec((HALO, POOL_WIDTH),
                         lambda t: (jnp.minimum((t + 1) * halo_per_tile, last_halo), 0)),
            full(ATTN_WIDTH),
            rows.mod_spec(layer, tile),
            pl.BlockSpec(band.shape, lambda t: (0, 0, 0)),
            _layer_spec(layer, *w_pool.shape[1:]),
            _layer_spec(layer, 1, POOL_WIDTH),
            _layer_spec(layer, *w_out.shape[1:]),
        ],
        out_specs=full(D_MODEL),
        out_shape=jax.ShapeDtypeStruct((n_rows, D_MODEL), F32),
        scratch_shapes=[pltpu.VMEM((tile, POOL_WIDTH), BF16)],
        compiler_params=_params("arbitrary"),
        name="mix",
    )(*streams, u, u, u, att, mods, band, w_pool, s_pool, w_out)


def _mlp_kernel(x_ref, mod_ref, g_ref, w1_ref, w2_ref, *rest):
    *gf_ref, o_ref = rest
    x = x_ref[...]
    h = _norm_mod(x, g_ref[...], mod_ref[3:4, :], mod_ref[4:5, :]).astype(BF16)
    acc = jnp.zeros(x.shape, F32)
    for c in range(D_FF // FF_CHUNK):
        cols = slice(c * FF_CHUNK, (c + 1) * FF_CHUNK)
        a = jnp.dot(h, w1_ref[:, cols], preferred_element_type=F32)
        a = jnp.square(jnp.maximum(a, 0.0)).astype(BF16)
        acc += jnp.dot(a, w2_ref[cols, :], preferred_element_type=F32)
    y = x + mod_ref[5:6, :] * acc
    if gf_ref:
        ms = jnp.mean(y * y, axis=-1, keepdims=True)
        y = y * lax.rsqrt(ms + EPS) * gf_ref[0][...]
    o_ref[...] = y


def _mlp_call(xs, rows, layer, mods, g, w1, w2, g_final=None):
    tile = MLP_TILE
    full = pl.BlockSpec((tile, D_MODEL), lambda t: (t, 0))
    extra, extra_specs = [], []
    if g_final is not None:
        extra, extra_specs = [g_final], [pl.BlockSpec((1, D_MODEL), lambda t: (0, 0))]
    return pl.pallas_call(
        _mlp_kernel,
        grid=(xs.shape[0] // tile,),
        in_specs=[
            full,
            rows.mod_spec(layer, tile),
            _layer_spec(layer, 1, D_MODEL),
            _layer_spec(layer, D_MODEL, D_FF),
            _layer_spec(layer, D_FF, D_MODEL),
        ] + extra_specs,
        out_specs=full,
        out_shape=jax.ShapeDtypeStruct(xs.shape, F32),
        compiler_params=_params("arbitrary"),
        name="mlp",
    )(xs, mods, g, w1, w2, *extra)


def _rope_tables(n_lat, pad_rows):
    grid_rows = n_lat // GRID_W
    half = HEAD_DIM // 2
    row = jnp.repeat(jnp.arange(grid_rows), GRID_W).astype(F32)
    col = jnp.tile(jnp.arange(GRID_W), grid_rows).astype(F32)
    inv = ROPE_BASE ** (-jnp.arange(0, half, 2, dtype=F32) / half)
    ang = jnp.concatenate([row[:, None] * inv, col[:, None] * inv], axis=-1)
    cos, sin = jnp.cos(ang), jnp.sin(ang)
    reps = LANES // HEAD_DIM
    cos_t = jnp.tile(jnp.concatenate([cos, cos], axis=-1), (1, reps))
    sin_t = jnp.tile(jnp.concatenate([-sin, sin], axis=-1), (1, reps))
    cos_t = jnp.concatenate([cos_t, jnp.ones((pad_rows, LANES), F32)], axis=0)
    sin_t = jnp.concatenate([sin_t, jnp.zeros((pad_rows, LANES), F32)], axis=0)
    return cos_t, sin_t


def _pool_bands():
    t = jnp.arange(POOL_SUB)[:, None]
    j = jnp.arange(POOL_SUB + 2 * HALO)[None, :]
    d = j - HALO - t
    return jnp.stack([((d >= -(w // 2)) & (d < w - w // 2)).astype(BF16) for w in POOL_WINDOWS])


def kernel(x, c, ctx, c_ctx, w_ada, b_ada, g_mix, g_mlp, w_in, w_pool, s_pool,
           lam_q1, lam_k1, lam_q2, lam_k2, g_subln, w_out, w_mlp1, w_mlp2, g_final):
    batch, n_lat, d_model = x.shape
    n_ctx = ctx.shape[1]
    depth = w_ada.shape[0]
    rows = _Rows(batch, n_lat, n_ctx)
    assert d_model == D_MODEL and batch < ADA_ROWS
    assert n_lat % ROW_TILE == 0 and rows.ctx % ROW_TILE == 0 and n_lat % n_ctx == 0
    assert n_lat % KV_CHUNK == 0 and n_lat % GRID_W == 0 and n_lat % Q_TILE == 0
    assert n_ctx % POOL_SUB == 0 and ROW_TILE % MLP_TILE == 0

    cvec = jnp.zeros((ADA_ROWS, D_MODEL), F32).at[:batch].set(c).at[batch].set(c_ctx)
    lam_init = [0.8 - 0.6 * math.exp(-0.3 * l) for l in range(depth)]
    linit = jnp.broadcast_to(jnp.asarray(lam_init, F32)[:, None, None], (depth, 1, LANES))
    vec3 = lambda a: a.reshape(depth, 1, a.shape[-1])
    mods, lam = _ada_call(cvec, w_ada, vec3(b_ada), vec3(lam_q1), vec3(lam_k1),
                          vec3(lam_q2), vec3(lam_k2), linit)
    mods = mods.reshape(depth, ADA_ROWS, N_ADA, D_MODEL)
    scal = jnp.stack([lam[:, 0, 0], 1.0 - jnp.asarray(lam_init, F32)], axis=-1)

    cos, sin = _rope_tables(n_lat, ROW_TILE)
    band = _pool_bands()
    w_in_b = w_in.astype(BF16)
    w_pool_b = w_pool.astype(BF16)
    w_out_b = w_out.astype(BF16)
    w1_b = w_mlp1.astype(BF16)
    w2_b = w_mlp2.astype(BF16)
    g_mix3, g_mlp3, g_sub3, s_pool3 = vec3(g_mix), vec3(g_mlp), vec3(g_subln), vec3(s_pool)

    streams = [x.reshape(rows.lat, D_MODEL), ctx.reshape(rows.ctx, D_MODEL)]
    for l in range(depth):
        with_ctx = l < depth - 1
        u, q, k, v = _inproj_call(streams, rows, l, mods, g_mix3, w_in_b, cos, sin)
        att = _attn_call(scal, q, k, v, g_sub3, rows, l, with_ctx)
        xs = _mix_call(streams, u, att, rows, l, mods, band, w_pool_b, s_pool3, w_out_b, with_ctx)
        xs = _mlp_call(xs, rows, l, mods, g_mlp3, w1_b, w2_b,
                       None if with_ctx else g_final[None])
        streams = [xs]
    return xs.reshape(batch, n_lat, D_MODEL)
```

```python
import functools
import math

import jax
import jax.numpy as jnp
from jax import lax
from jax.experimental import pallas as pl
from jax.experimental.pallas import tpu as pltpu

D_MODEL = 1024
GRID_W = 64
POOL_WIDTH = 512
POOL_WINDOWS = (2, 4, 8, 16)
POOL_GC = 128
N_HEADS = 4
HEAD_DIM = 64
V_DIM = 128
QK_WIDTH = 512
ATTN_WIDTH = 512
IN_WIDTH = 2048
ROPE_BASE = 10000.0
D_FF = 4096
N_ADA = 6
EPS = 1e-6

LANES = 128
ADA_ROWS = 8
ROW_TILE = 1024
POST_TILE = 512
Q_TILE = 1024
POOL_PIECE = 256
HALO = 16
KV_CHUNK = 256
FF_CHUNK = 1024
ADA_COLS = 1024
VMEM_LIMIT = 56 * 1024 * 1024

F32 = jnp.float32
BF16 = jnp.bfloat16


def _params(*sem):
    return pltpu.CompilerParams(dimension_semantics=sem, vmem_limit_bytes=VMEM_LIMIT)


class _Rows:
    def __init__(self, batch, n_lat, n_ctx):
        self.batch, self.n_lat, self.n_ctx = batch, n_lat, n_ctx
        self.lat = batch * n_lat
        self.ctx = batch * n_ctx
        self.all = self.lat + self.ctx

    def mod_spec(self, layer, tile):
        lat_tiles, per_batch = self.lat // tile, self.n_lat // tile
        return pl.BlockSpec(
            (None, None, N_ADA, D_MODEL),
            lambda t: (layer, jnp.where(t < lat_tiles, t // per_batch, self.batch), 0, 0))

    def stream_specs(self, tile, width, split):
        if not split:
            return [pl.BlockSpec((tile, width), lambda t: (t, 0))]
        lat_tiles = self.lat // tile
        return [pl.BlockSpec((tile, width), lambda t: (jnp.minimum(t, lat_tiles - 1), 0)),
                pl.BlockSpec((tile, width), lambda t: (jnp.maximum(t - lat_tiles, 0), 0))]


def _stream_rows(refs, lat_tiles):
    if len(refs) == 1:
        return refs[0][...]
    return jnp.where(pl.program_id(0) >= lat_tiles, refs[1][...], refs[0][...])


def _layer_spec(layer, *shape):
    return pl.BlockSpec((None,) + shape, lambda t: (layer,) + (0,) * len(shape))


def _norm_mod(x, g, shift, scale):
    ms = jnp.mean(x * x, axis=-1, keepdims=True)
    return (x * lax.rsqrt(ms + EPS) * g) * (1.0 + scale) + shift


def _ada_kernel(c_ref, w_ref, b_ref, lq1_ref, lk1_ref, lq2_ref, lk2_ref, linit_ref,
                mod_ref, lam_ref):
    c = c_ref[...]
    s = c / (1.0 + jnp.exp(-c))
    mod_ref[0] = jnp.dot(s, w_ref[0], preferred_element_type=F32,
                         precision=lax.Precision.HIGHEST) + b_ref[0]
    d1 = jnp.sum(lq1_ref[0] * lk1_ref[0], axis=-1, keepdims=True)
    d2 = jnp.sum(lq2_ref[0] * lk2_ref[0], axis=-1, keepdims=True)
    lam = jnp.exp(d1) - jnp.exp(d2) + linit_ref[0]
    lam_ref[0] = jnp.broadcast_to(lam, (ADA_ROWS, LANES))


def _ada_call(cvec, w_ada, b_ada, lq1, lk1, lq2, lk2, linit):
    depth = w_ada.shape[0]
    width = w_ada.shape[2]
    vec = lambda: pl.BlockSpec((1, 1, HEAD_DIM), lambda l, j: (l, 0, 0))
    return pl.pallas_call(
        _ada_kernel,
        grid=(depth, width // ADA_COLS),
        in_specs=[
            pl.BlockSpec((ADA_ROWS, D_MODEL), lambda l, j: (0, 0)),
            pl.BlockSpec((1, D_MODEL, ADA_COLS), lambda l, j: (l, 0, j)),
            pl.BlockSpec((1, 1, ADA_COLS), lambda l, j: (l, 0, j)),
            vec(), vec(), vec(), vec(),
            pl.BlockSpec((1, 1, LANES), lambda l, j: (l, 0, 0)),
        ],
        out_specs=[
            pl.BlockSpec((1, ADA_ROWS, ADA_COLS), lambda l, j: (l, 0, j)),
            pl.BlockSpec((1, ADA_ROWS, LANES), lambda l, j: (l, 0, 0)),
        ],
        out_shape=[
            jax.ShapeDtypeStruct((depth, ADA_ROWS, width), F32),
            jax.ShapeDtypeStruct((depth, ADA_ROWS, LANES), F32),
        ],
        compiler_params=_params("arbitrary", "arbitrary"),
        name="ada",
    )(cvec, w_ada, b_ada, lq1, lk1, lq2, lk2, linit)


def _inproj_kernel(*refs, lat_tiles):
    *x_refs, mod_ref, g_ref, w_ref, cos_ref, sin_ref, u_ref, q_ref, k_ref, v_ref = refs
    x = _stream_rows(x_refs, lat_tiles)
    h = _norm_mod(x, g_ref[...], mod_ref[0:1, :], mod_ref[1:2, :]).astype(BF16)
    cos = cos_ref[...]
    sin = sin_ref[...]
    lane = lax.broadcasted_iota(jnp.int32, cos.shape, 1)
    first_half = (lane & (HEAD_DIM // 2)) == 0
    q_scale = HEAD_DIM ** -0.5 * math.log2(math.e)

    def rope(t):
        partner = jnp.where(first_half,
                            pltpu.roll(t, LANES - HEAD_DIM // 2, 1),
                            pltpu.roll(t, HEAD_DIM // 2, 1))
        return t * cos + partner * sin

    def proj(lo, width):
        return jnp.dot(h, w_ref[:, lo:lo + width], preferred_element_type=F32)

    u_ref[...] = proj(0, POOL_WIDTH)
    z = proj(POOL_WIDTH, QK_WIDTH)
    for c in range(QK_WIDTH // LANES):
        cols = slice(c * LANES, (c + 1) * LANES)
        q_ref[:, cols] = (rope(z[:, cols]) * q_scale).astype(BF16)
    z = proj(POOL_WIDTH + QK_WIDTH, QK_WIDTH)
    for c in range(QK_WIDTH // LANES):
        cols = slice(c * LANES, (c + 1) * LANES)
        k_ref[:, cols] = rope(z[:, cols]).astype(BF16)
    z = proj(POOL_WIDTH + 2 * QK_WIDTH, ATTN_WIDTH)
    ones = jnp.ones((z.shape[0], V_DIM), BF16)
    for hd in range(N_HEADS):
        v_ref[:, 2 * hd * V_DIM:(2 * hd + 1) * V_DIM] = z[:, hd * V_DIM:(hd + 1) * V_DIM].astype(BF16)
        v_ref[:, (2 * hd + 1) * V_DIM:(2 * hd + 2) * V_DIM] = ones


def _inproj_call(streams, rows, layer, mods, g, w_in, cos, sin):
    tile = ROW_TILE
    lat_tiles, per_batch = rows.lat // tile, rows.n_lat // tile
    split = len(streams) == 2
    table = pl.BlockSpec((tile, LANES),
                         lambda t: (jnp.where(t < lat_tiles, t % per_batch, per_batch), 0))
    out = lambda width: pl.BlockSpec((tile, width), lambda t: (t, 0))
    shape = lambda width, dt: jax.ShapeDtypeStruct((rows.all, width), dt)
    return pl.pallas_call(
        functools.partial(_inproj_kernel, lat_tiles=lat_tiles),
        grid=(rows.all // tile,),
        in_specs=rows.stream_specs(tile, D_MODEL, split) + [
            rows.mod_spec(layer, tile),
            _layer_spec(layer, 1, D_MODEL),
            _layer_spec(layer, D_MODEL, IN_WIDTH),
            table, table,
        ],
        out_specs=[out(POOL_WIDTH), out(QK_WIDTH), out(QK_WIDTH), out(2 * ATTN_WIDTH)],
        out_shape=[shape(POOL_WIDTH, F32), shape(QK_WIDTH, BF16), shape(QK_WIDTH, BF16),
                   shape(2 * ATTN_WIDTH, BF16)],
        compiler_params=_params("arbitrary"),
        name="inproj",
    )(*streams, mods, g, w_in, cos, sin)


def _attn_kernel(scal_ref, q_ref, *refs, layer, latent):
    if latent:
        kl_ref, vl_ref, kc_ref, vc_ref, g_ref, o_ref = refs
    else:
        kc_ref, vc_ref, g_ref, _, o_ref = refs
    tq = q_ref.shape[0]
    q = q_ref[...]
    lane = lax.broadcasted_iota(jnp.int32, q.shape, 1)
    zero = jnp.zeros_like(q)
    qq = jnp.concatenate([jnp.where(lane < HEAD_DIM, q, zero),
                          jnp.where(lane >= HEAD_DIM, q, zero)], axis=0)

    def step(carry, kc, vc):
        m, acc = carry
        s = lax.dot_general(qq, kc, (((1,), (1,)), ((), ())), preferred_element_type=F32)
        m_new = jnp.maximum(m, jnp.max(s, axis=-1, keepdims=True))
        p = jnp.exp2(s - m_new).astype(BF16)
        pv = jnp.dot(p, vc, preferred_element_type=F32)
        if acc is None:
            return m_new, pv
        return m_new, jnp.exp2(m - m_new) * acc + pv

    carry = (jnp.full((2 * tq, 1), -jnp.inf, F32), None)
    if latent:
        for c in range(kl_ref.shape[0] // KV_CHUNK):
            rows = slice(c * KV_CHUNK, (c + 1) * KV_CHUNK)
            carry = step(carry, kl_ref[rows, :], vl_ref[rows, :])
    acc = step(carry, kc_ref[...], vc_ref[...])[1]
    o = acc[:, :V_DIM] / acc[:, V_DIM:]
    o = o[:tq] - scal_ref[layer, 0] * o[tq:]
    ms = jnp.mean(o * o, axis=-1, keepdims=True)
    o_ref[...] = (o * lax.rsqrt(ms + EPS) * g_ref[...] * scal_ref[layer, 1]).astype(o_ref.dtype)


def _attn_call(scal, q, k, v, g, rows, layer, with_ctx):
    ctx_blocks = rows.lat // rows.n_ctx
    lat_tiles = rows.n_lat // Q_TILE
    smem = pl.BlockSpec(memory_space=pltpu.SMEM)
    lat = lambda width: pl.BlockSpec((rows.n_lat, width), lambda b, h, *i: (b, h))
    ctx = lambda width: pl.BlockSpec((rows.n_ctx, width), lambda b, h, *i: (ctx_blocks + b, h))
    gain = pl.BlockSpec((None, 1, V_DIM), lambda b, h, *i: (layer, 0, 0))
    out_shape = jax.ShapeDtypeStruct((rows.all if with_ctx else rows.lat, ATTN_WIDTH), BF16)
    q_lat = pl.BlockSpec((Q_TILE, V_DIM), lambda b, h, i: (b * lat_tiles + i, h))
    att = pl.pallas_call(
        functools.partial(_attn_kernel, layer=layer, latent=True),
        grid=(rows.batch, N_HEADS, lat_tiles),
        in_specs=[smem, q_lat, lat(V_DIM), lat(2 * V_DIM), ctx(V_DIM), ctx(2 * V_DIM), gain],
        out_specs=q_lat,
        out_shape=out_shape,
        compiler_params=_params("arbitrary", "arbitrary", "arbitrary"),
        name="attn",
    )(scal, q, k, v, k, v, g)
    if not with_ctx:
        return att
    return pl.pallas_call(
        functools.partial(_attn_kernel, layer=layer, latent=False),
        grid=(rows.batch, N_HEADS),
        in_specs=[smem, ctx(V_DIM), ctx(V_DIM), ctx(2 * V_DIM), gain,
                  pl.BlockSpec(memory_space=pl.ANY)],
        out_specs=ctx(V_DIM),
        out_shape=out_shape,
        input_output_aliases={5: 0},
        compiler_params=_params("arbitrary", "arbitrary"),
        name="attn_ctx",
    )(scal, q, k, v, g, att)


def _fold_kernel(wp_ref, sp_ref, wo_ref, o_ref):
    r = pl.program_id(1)

    @pl.when(r < POOL_WIDTH // POOL_GC)
    def _():
        o_ref[...] = jnp.dot(wp_ref[...] * sp_ref[...], wo_ref[...], preferred_element_type=F32,
                             precision=lax.Precision.HIGHEST).astype(o_ref.dtype)

    @pl.when(r >= POOL_WIDTH // POOL_GC)
    def _():
        o_ref[...] = wo_ref[...].astype(o_ref.dtype)


def _fold_call(w_pool, s_pool, w_out):
    depth, mix_width, d_model = w_out.shape
    groups = POOL_WIDTH // POOL_GC
    blk = pl.BlockSpec((None, POOL_GC, d_model), lambda l, r: (l, r, 0))
    return pl.pallas_call(
        _fold_kernel,
        grid=(depth, mix_width // POOL_GC),
        in_specs=[
            pl.BlockSpec((None, None, POOL_GC, POOL_GC),
                         lambda l, r: (l, jnp.minimum(r, groups - 1), 0, 0)),
            pl.BlockSpec((None, 1, POOL_GC), lambda l, r: (l, 0, jnp.minimum(r, groups - 1))),
            blk,
        ],
        out_specs=blk,
        out_shape=jax.ShapeDtypeStruct(w_out.shape, BF16),
        compiler_params=_params("arbitrary", "arbitrary"),
        name="fold",
    )(w_pool, s_pool, w_out)


def _window_sum(ext, w):
    n = ext.shape[0]
    back = lambda a, k: pltpu.roll(a, n - k, 0)
    a, span = ext, 1
    while 2 * span < w:
        a = a + back(a, span)
        span *= 2
    half = w // 2
    tot = pltpu.roll(a, half, 0) + a
    return tot[HALO:n - HALO]


def _post_kernel(*refs, rows, tile, n_x, final):
    x_refs, refs = refs[:n_x], refs[n_x:]
    (u_ref, up_ref, un_ref, att_ref, mod_ref, wm_ref, g_ref, w1_ref, w2_ref, *gf_ref,
     o_ref, y_ref) = refs
    lat_tiles = rows.lat // tile
    r0 = pl.program_id(0) * tile
    in_ctx = r0 >= rows.lat
    seg_len = jnp.where(in_ctx, rows.n_ctx, rows.n_lat)
    seg_base = jnp.where(in_ctx, rows.lat, 0)
    pieces = tile // POOL_PIECE
    for j in range(pieces):
        lo, hi = j * POOL_PIECE, (j + 1) * POOL_PIECE
        pos = lax.rem(r0 + lo - seg_base, seg_len)
        before = up_ref[...] if j == 0 else u_ref[lo - HALO:lo, :]
        after = un_ref[...] if j == pieces - 1 else u_ref[hi:hi + HALO, :]
        before = jnp.where(pos > 0, before, 0.0)
        after = jnp.where(pos + POOL_PIECE < seg_len, after, 0.0)
        cur = u_ref[lo:hi, :]
        ext = jnp.concatenate([before, cur, after], axis=0)
        trow = pos + lax.broadcasted_iota(jnp.int32, (POOL_PIECE, 1), 0)
        for gi, w in enumerate(POOL_WINDOWS):
            cols = slice(gi * POOL_GC, (gi + 1) * POOL_GC)
            cnt = jnp.minimum(trow + (w - w // 2), seg_len) - jnp.maximum(trow - w // 2, 0)
            y = _window_sum(ext[:, cols], w) / cnt.astype(F32) - cur[:, cols]
            y_ref[lo:hi, cols] = y.astype(BF16)
    mix = jnp.dot(y_ref[...], wm_ref[:POOL_WIDTH, :], preferred_element_type=F32)
    mix += jnp.dot(att_ref[...], wm_ref[POOL_WIDTH:, :], preferred_element_type=F32)
    x = _stream_rows(x_refs, lat_tiles) + mod_ref[2:3, :] * mix

    h = _norm_mod(x, g_ref[...], mod_ref[3:4, :], mod_ref[4:5, :]).astype(BF16)
    acc = jnp.zeros(x.shape, F32)
    for c in range(D_FF // FF_CHUNK):
        cols = slice(c * FF_CHUNK, (c + 1) * FF_CHUNK)
        a = jnp.dot(h, w1_ref[:, cols], preferred_element_type=F32)
        a = jnp.square(jnp.maximum(a, 0.0)).astype(BF16)
        acc += jnp.dot(a, w2_ref[cols, :], preferred_element_type=F32)
    x = x + mod_ref[5:6, :] * acc
    if final:
        ms = jnp.mean(x * x, axis=-1, keepdims=True)
        x = x * lax.rsqrt(ms + EPS) * gf_ref[0][...]
    o_ref[...] = x


def _post_call(streams, u, att, rows, layer, mods, w_mix, g, w1, w2, g_final=None):
    tile = POST_TILE
    final = g_final is not None
    n_rows = rows.lat if final else rows.all
    halo_per_tile = tile // HALO
    last_halo = rows.all // HALO - 1
    full = lambda width: pl.BlockSpec((tile, width), lambda t: (t, 0))
    weights = lambda *shape: pl.BlockSpec((None,) + shape, lambda t: (layer,) + (0,) * len(shape),
                                          pipeline_mode=pl.Buffered(1))
    extra, extra_specs = [], []
    if final:
        extra, extra_specs = [g_final], [pl.BlockSpec((1, D_MODEL), lambda t: (0, 0))]
    return pl.pallas_call(
        functools.partial(_post_kernel, rows=rows, tile=tile, n_x=len(streams), final=final),
        grid=(n_rows // tile,),
        in_specs=rows.stream_specs(tile, D_MODEL, len(streams) == 2) + [
            full(POOL_WIDTH),
            pl.BlockSpec((HALO, POOL_WIDTH), lambda t: (jnp.maximum(t * halo_per_tile - 1, 0), 0)),
            pl.BlockSpec((HALO, POOL_WIDTH),
                         lambda t: (jnp.minimum((t + 1) * halo_per_tile, last_halo), 0)),
            full(ATTN_WIDTH),
            rows.mod_spec(layer, tile),
            weights(*w_mix.shape[1:]),
            _layer_spec(layer, 1, D_MODEL),
            weights(D_MODEL, D_FF),
            weights(D_FF, D_MODEL),
        ] + extra_specs,
        out_specs=full(D_MODEL),
        out_shape=jax.ShapeDtypeStruct((n_rows, D_MODEL), F32),
        scratch_shapes=[pltpu.VMEM((tile, POOL_WIDTH), BF16)],
        compiler_params=_params("arbitrary"),
        name="post",
    )(*streams, u, u, u, att, mods, w_mix, g, w1, w2, *extra)


def _rope_tables(n_lat, pad_rows):
    grid_rows = n_lat // GRID_W
    half = HEAD_DIM // 2
    row = jnp.repeat(jnp.arange(grid_rows), GRID_W).astype(F32)
    col = jnp.tile(jnp.arange(GRID_W), grid_rows).astype(F32)
    inv = ROPE_BASE ** (-jnp.arange(0, half, 2, dtype=F32) / half)
    ang = jnp.concatenate([row[:, None] * inv, col[:, None] * inv], axis=-1)
    cos, sin = jnp.cos(ang), jnp.sin(ang)
    reps = LANES // HEAD_DIM
    cos_t = jnp.tile(jnp.concatenate([cos, cos], axis=-1), (1, reps))
    sin_t = jnp.tile(jnp.concatenate([-sin, sin], axis=-1), (1, reps))
    cos_t = jnp.concatenate([cos_t, jnp.ones((pad_rows, LANES), F32)], axis=0)
    sin_t = jnp.concatenate([sin_t, jnp.zeros((pad_rows, LANES), F32)], axis=0)
    return cos_t, sin_t


def kernel(x, c, ctx, c_ctx, w_ada, b_ada, g_mix, g_mlp, w_in, w_pool, s_pool,
           lam_q1, lam_k1, lam_q2, lam_k2, g_subln, w_out, w_mlp1, w_mlp2, g_final):
    batch, n_lat, d_model = x.shape
    n_ctx = ctx.shape[1]
    depth = w_ada.shape[0]
    rows = _Rows(batch, n_lat, n_ctx)
    assert d_model == D_MODEL and batch < ADA_ROWS
    assert n_lat % ROW_TILE == 0 and rows.ctx % ROW_TILE == 0 and n_lat % n_ctx == 0
    assert n_lat % KV_CHUNK == 0 and n_lat % GRID_W == 0 and n_lat % Q_TILE == 0
    assert n_ctx % POOL_PIECE == 0 and ROW_TILE % POST_TILE == 0 and POST_TILE % POOL_PIECE == 0

    cvec = jnp.zeros((ADA_ROWS, D_MODEL), F32).at[:batch].set(c).at[batch].set(c_ctx)
    lam_init = [0.8 - 0.6 * math.exp(-0.3 * l) for l in range(depth)]
    linit = jnp.broadcast_to(jnp.asarray(lam_init, F32)[:, None, None], (depth, 1, LANES))
    vec3 = lambda a: a.reshape(depth, 1, a.shape[-1])
    mods, lam = _ada_call(cvec, w_ada, vec3(b_ada), vec3(lam_q1), vec3(lam_k1),
                          vec3(lam_q2), vec3(lam_k2), linit)
    mods = mods.reshape(depth, ADA_ROWS, N_ADA, D_MODEL)
    scal = jnp.stack([lam[:, 0, 0], 1.0 - jnp.asarray(lam_init, F32)], axis=-1)

    cos, sin = _rope_tables(n_lat, ROW_TILE)
    w_in_b = w_in.astype(BF16)
    w1_b = w_mlp1.astype(BF16)
    w2_b = w_mlp2.astype(BF16)
    g_mix3, g_mlp3, g_sub3 = vec3(g_mix), vec3(g_mlp), vec3(g_subln)
    w_mix_b = _fold_call(w_pool, vec3(s_pool), w_out)

    streams = [x.reshape(rows.lat, D_MODEL), ctx.reshape(rows.ctx, D_MODEL)]
    for l in range(depth):
        with_ctx = l < depth - 1
        u, q, k, v = _inproj_call(streams, rows, l, mods, g_mix3, w_in_b, cos, sin)
        att = _attn_call(scal, q, k, v, g_sub3, rows, l, with_ctx)
        xs = _post_call(streams, u, att, rows, l, mods, w_mix_b, g_mlp3, w1_b, w2_b,
                        None if with_ctx else g_final[None])
        streams = [xs]
    return xs.reshape(batch, n_lat, D_MODEL)
```

```python
import functools
import math

import jax
import jax.numpy as jnp
from jax import lax
from jax.experimental import pallas as pl
from jax.experimental.pallas import tpu as pltpu

D_MODEL = 1024
GRID_W = 64
POOL_WIDTH = 512
POOL_WINDOWS = (2, 4, 8, 16)
POOL_GC = 128
N_HEADS = 4
HEAD_DIM = 64
V_DIM = 128
QK_WIDTH = 512
ATTN_WIDTH = 512
IN_WIDTH = 2048
ROPE_BASE = 10000.0
D_FF = 4096
N_ADA = 6
EPS = 1e-6

LANES = 128
ADA_ROWS = 8
ROW_TILE = 1024
POST_TILE = 512
Q_TILE = 1024
POOL_PIECE = 256
HALO = 16
KV_CHUNK = 256
FF_CHUNK = 1024
ADA_COLS = 1024
VMEM_LIMIT = 56 * 1024 * 1024

F32 = jnp.float32
BF16 = jnp.bfloat16


def _params(*sem):
    return pltpu.CompilerParams(dimension_semantics=sem, vmem_limit_bytes=VMEM_LIMIT)


class _Rows:
    def __init__(self, batch, n_lat, n_ctx):
        self.batch, self.n_lat, self.n_ctx = batch, n_lat, n_ctx
        self.lat = batch * n_lat
        self.ctx = batch * n_ctx
        self.all = self.lat + self.ctx

    def mod_spec(self, layer, tile):
        lat_tiles, per_batch = self.lat // tile, self.n_lat // tile
        return pl.BlockSpec(
            (None, None, N_ADA, D_MODEL),
            lambda t: (layer, jnp.where(t < lat_tiles, t // per_batch, self.batch), 0, 0))

    def stream_specs(self, tile, width, split):
        if not split:
            return [pl.BlockSpec((tile, width), lambda t: (t, 0))]
        lat_tiles = self.lat // tile
        return [pl.BlockSpec((tile, width), lambda t: (jnp.minimum(t, lat_tiles - 1), 0)),
                pl.BlockSpec((tile, width), lambda t: (jnp.maximum(t - lat_tiles, 0), 0))]


def _stream_rows(refs, lat_tiles):
    if len(refs) == 1:
        return refs[0][...]
    return jnp.where(pl.program_id(0) >= lat_tiles, refs[1][...], refs[0][...])


def _layer_spec(layer, *shape):
    return pl.BlockSpec((None,) + shape, lambda t: (layer,) + (0,) * len(shape))


def _norm_mod(x, g, shift, scale):
    ms = jnp.mean(x * x, axis=-1, keepdims=True)
    return (x * lax.rsqrt(ms + EPS) * g) * (1.0 + scale) + shift


def _ada_kernel(c_ref, w_ref, b_ref, lq1_ref, lk1_ref, lq2_ref, lk2_ref, linit_ref,
                mod_ref, lam_ref):
    c = c_ref[...]
    s = c / (1.0 + jnp.exp(-c))
    mod_ref[0] = jnp.dot(s, w_ref[0], preferred_element_type=F32,
                         precision=lax.Precision.HIGHEST) + b_ref[0]
    d1 = jnp.sum(lq1_ref[0] * lk1_ref[0], axis=-1, keepdims=True)
    d2 = jnp.sum(lq2_ref[0] * lk2_ref[0], axis=-1, keepdims=True)
    lam = jnp.exp(d1) - jnp.exp(d2) + linit_ref[0]
    lam_ref[0] = jnp.broadcast_to(lam, (ADA_ROWS, LANES))


def _ada_call(cvec, w_ada, b_ada, lq1, lk1, lq2, lk2, linit):
    depth = w_ada.shape[0]
    width = w_ada.shape[2]
    vec = lambda: pl.BlockSpec((1, 1, HEAD_DIM), lambda l, j: (l, 0, 0))
    return pl.pallas_call(
        _ada_kernel,
        grid=(depth, width // ADA_COLS),
        in_specs=[
            pl.BlockSpec((ADA_ROWS, D_MODEL), lambda l, j: (0, 0)),
            pl.BlockSpec((1, D_MODEL, ADA_COLS), lambda l, j: (l, 0, j)),
            pl.BlockSpec((1, 1, ADA_COLS), lambda l, j: (l, 0, j)),
            vec(), vec(), vec(), vec(),
            pl.BlockSpec((1, 1, LANES), lambda l, j: (l, 0, 0)),
        ],
        out_specs=[
            pl.BlockSpec((1, ADA_ROWS, ADA_COLS), lambda l, j: (l, 0, j)),
            pl.BlockSpec((1, ADA_ROWS, LANES), lambda l, j: (l, 0, 0)),
        ],
        out_shape=[
            jax.ShapeDtypeStruct((depth, ADA_ROWS, width), F32),
            jax.ShapeDtypeStruct((depth, ADA_ROWS, LANES), F32),
        ],
        compiler_params=_params("arbitrary", "arbitrary"),
        name="ada",
    )(cvec, w_ada, b_ada, lq1, lk1, lq2, lk2, linit)


def _inproj_kernel(*refs, lat_tiles):
    *x_refs, mod_ref, g_ref, w_ref, cos_ref, sin_ref, u_ref, q_ref, k_ref, v_ref = refs
    x = _stream_rows(x_refs, lat_tiles)
    h = _norm_mod(x, g_ref[...], mod_ref[0:1, :], mod_ref[1:2, :]).astype(BF16)
    cos = cos_ref[...]
    sin = sin_ref[...]
    lane = lax.broadcasted_iota(jnp.int32, cos.shape, 1)
    first_half = (lane & (HEAD_DIM // 2)) == 0
    q_scale = HEAD_DIM ** -0.5 * math.log2(math.e)

    def rope(t):
        partner = jnp.where(first_half,
                            pltpu.roll(t, LANES - HEAD_DIM // 2, 1),
                            pltpu.roll(t, HEAD_DIM // 2, 1))
        return t * cos + partner * sin

    def proj(lo, width):
        return jnp.dot(h, w_ref[:, lo:lo + width], preferred_element_type=F32)

    u_ref[...] = proj(0, POOL_WIDTH)
    z = proj(POOL_WIDTH, QK_WIDTH)
    for c in range(QK_WIDTH // LANES):
        cols = slice(c * LANES, (c + 1) * LANES)
        q_ref[:, cols] = (rope(z[:, cols]) * q_scale).astype(BF16)
    z = proj(POOL_WIDTH + QK_WIDTH, QK_WIDTH)
    for c in range(QK_WIDTH // LANES):
        cols = slice(c * LANES, (c + 1) * LANES)
        k_ref[:, cols] = rope(z[:, cols]).astype(BF16)
    z = proj(POOL_WIDTH + 2 * QK_WIDTH, ATTN_WIDTH)
    ones = jnp.ones((z.shape[0], V_DIM), BF16)
    for hd in range(N_HEADS):
        v_ref[:, 2 * hd * V_DIM:(2 * hd + 1) * V_DIM] = z[:, hd * V_DIM:(hd + 1) * V_DIM].astype(BF16)
        v_ref[:, (2 * hd + 1) * V_DIM:(2 * hd + 2) * V_DIM] = ones


def _inproj_call(streams, rows, layer, mods, g, w_in, cos, sin):
    tile = ROW_TILE
    lat_tiles, per_batch = rows.lat // tile, rows.n_lat // tile
    split = len(streams) == 2
    table = pl.BlockSpec((tile, LANES),
                         lambda t: (jnp.where(t < lat_tiles, t % per_batch, per_batch), 0))
    out = lambda width: pl.BlockSpec((tile, width), lambda t: (t, 0))
    shape = lambda width, dt: jax.ShapeDtypeStruct((rows.all, width), dt)
    return pl.pallas_call(
        functools.partial(_inproj_kernel, lat_tiles=lat_tiles),
        grid=(rows.all // tile,),
        in_specs=rows.stream_specs(tile, D_MODEL, split) + [
            rows.mod_spec(layer, tile),
            _layer_spec(layer, 1, D_MODEL),
            pl.BlockSpec(w_in.shape, lambda t: (0, 0)),
            table, table,
        ],
        out_specs=[out(POOL_WIDTH), out(QK_WIDTH), out(QK_WIDTH), out(2 * ATTN_WIDTH)],
        out_shape=[shape(POOL_WIDTH, F32), shape(QK_WIDTH, BF16), shape(QK_WIDTH, BF16),
                   shape(2 * ATTN_WIDTH, BF16)],
        compiler_params=_params("arbitrary"),
        name="inproj",
    )(*streams, mods, g, w_in, cos, sin)


def _diff_attention(q, key_value_blocks, lam, gain):
    tq = q.shape[0]
    lane = lax.broadcasted_iota(jnp.int32, q.shape, 1)
    zero = jnp.zeros_like(q)
    qq = jnp.concatenate([jnp.where(lane < HEAD_DIM, q, zero),
                          jnp.where(lane >= HEAD_DIM, q, zero)], axis=0)

    def step(carry, kc, vc):
        m, acc = carry
        s = lax.dot_general(qq, kc, (((1,), (1,)), ((), ())), preferred_element_type=F32)
        m_new = jnp.maximum(m, jnp.max(s, axis=-1, keepdims=True))
        p = jnp.exp2(s - m_new).astype(BF16)
        pv = jnp.dot(p, vc, preferred_element_type=F32)
        if acc is None:
            return m_new, pv
        return m_new, jnp.exp2(m - m_new) * acc + pv

    carry = (jnp.full((2 * tq, 1), -jnp.inf, F32), None)
    for block in key_value_blocks:
        carry = step(carry, *block())
    acc = carry[1]
    o = acc[:, :V_DIM] / acc[:, V_DIM:]
    o = o[:tq] - lam * o[tq:]
    ms = jnp.mean(o * o, axis=-1, keepdims=True)
    return o * lax.rsqrt(ms + EPS) * gain


def _attn_kernel(scal_ref, q_ref, kl_ref, vl_ref, kc_ref, vc_ref, g_ref, *rest, layer):
    n_cast = len(rest) // 2
    cast_src, o_ref, cast_dst = rest[:n_cast], rest[n_cast], rest[n_cast + 1:]
    blocks = [functools.partial(lambda rows: (kl_ref[rows, :], vl_ref[rows, :]),
                                slice(c * KV_CHUNK, (c + 1) * KV_CHUNK))
              for c in range(kl_ref.shape[0] // KV_CHUNK)]
    blocks.append(lambda: (kc_ref[...], vc_ref[...]))
    o_ref[...] = _diff_attention(q_ref[...], blocks, scal_ref[layer, 0],
                                 g_ref[...] * scal_ref[layer, 1]).astype(o_ref.dtype)
    for src, dst in zip(cast_src, cast_dst):
        dst[...] = src[...].astype(dst.dtype)


def _attn_ctx_kernel(scal_ref, q_ref, k_ref, v_ref, g_ref, o_ref, *, layer):
    gain = g_ref[...] * scal_ref[layer, 1]
    for hd in range(N_HEADS):
        cols = slice(hd * V_DIM, (hd + 1) * V_DIM)
        vcols = slice(2 * hd * V_DIM, 2 * (hd + 1) * V_DIM)
        block = functools.partial(lambda c, vc: (k_ref[:, c], v_ref[:, vc]), cols, vcols)
        o_ref[:, cols] = _diff_attention(q_ref[:, cols], [block], scal_ref[layer, 0],
                                         gain).astype(o_ref.dtype)


def _attn_call(scal, q, k, v, g, rows, layer, with_ctx, casts):
    ctx_blocks = rows.lat // rows.n_ctx
    lat_tiles = rows.n_lat // Q_TILE
    steps = rows.batch * N_HEADS * lat_tiles
    smem = pl.BlockSpec(memory_space=pltpu.SMEM)
    lat = lambda width: pl.BlockSpec((rows.n_lat, width), lambda b, h, i: (b, h))
    ctx = lambda width: pl.BlockSpec((rows.n_ctx, width), lambda b, h, i: (ctx_blocks + b, h))
    gain = pl.BlockSpec((None, 1, V_DIM), lambda *_: (layer, 0, 0))
    q_lat = pl.BlockSpec((Q_TILE, V_DIM), lambda b, h, i: (b * lat_tiles + i, h))
    step = lambda b, h, i: (b * N_HEADS + h) * lat_tiles + i
    slab_in = lambda w, l: pl.BlockSpec((None, w.shape[1] // steps, w.shape[2]),
                                        lambda b, h, i: (l, step(b, h, i), 0))
    slab_out = lambda w: pl.BlockSpec((w.shape[1] // steps, w.shape[2]),
                                      lambda b, h, i: (step(b, h, i), 0))
    att, *cast_out = pl.pallas_call(
        functools.partial(_attn_kernel, layer=layer),
        grid=(rows.batch, N_HEADS, lat_tiles),
        in_specs=[smem, q_lat, lat(V_DIM), lat(2 * V_DIM), ctx(V_DIM), ctx(2 * V_DIM), gain]
        + [slab_in(w, l) for w, l in casts],
        out_specs=[q_lat] + [slab_out(w) for w, _ in casts],
        out_shape=[jax.ShapeDtypeStruct((rows.lat, ATTN_WIDTH), BF16)]
        + [jax.ShapeDtypeStruct(w.shape[1:], BF16) for w, _ in casts],
        compiler_params=_params("arbitrary", "arbitrary", "arbitrary"),
        name="attn",
    )(scal, q, k, v, k, v, g, *[w for w, _ in casts])
    if not with_ctx:
        return att, None, cast_out
    att_ctx = pl.pallas_call(
        functools.partial(_attn_ctx_kernel, layer=layer),
        grid=(rows.batch,),
        in_specs=[smem] + [pl.BlockSpec((rows.n_ctx, width), lambda b: (ctx_blocks + b, 0))
                           for width in (QK_WIDTH, QK_WIDTH, 2 * ATTN_WIDTH)] + [gain],
        out_specs=pl.BlockSpec((rows.n_ctx, ATTN_WIDTH), lambda b: (b, 0)),
        out_shape=jax.ShapeDtypeStruct((rows.ctx, ATTN_WIDTH), BF16),
        compiler_params=_params("arbitrary"),
        name="attn_ctx",
    )(scal, q, k, v, g)
    return att, att_ctx, cast_out


def _fold_kernel(wp_ref, sp_ref, wo_ref, o_ref):
    r = pl.program_id(1)

    @pl.when(r < POOL_WIDTH // POOL_GC)
    def _():
        o_ref[...] = jnp.dot(wp_ref[...] * sp_ref[...], wo_ref[...], preferred_element_type=F32,
                             precision=lax.Precision.HIGHEST).astype(o_ref.dtype)

    @pl.when(r >= POOL_WIDTH // POOL_GC)
    def _():
        o_ref[...] = wo_ref[...].astype(o_ref.dtype)


def _fold_call(w_pool, s_pool, w_out):
    depth, mix_width, d_model = w_out.shape
    groups = POOL_WIDTH // POOL_GC
    blk = pl.BlockSpec((None, POOL_GC, d_model), lambda l, r: (l, r, 0))
    return pl.pallas_call(
        _fold_kernel,
        grid=(depth, mix_width // POOL_GC),
        in_specs=[
            pl.BlockSpec((None, None, POOL_GC, POOL_GC),
                         lambda l, r: (l, jnp.minimum(r, groups - 1), 0, 0)),
            pl.BlockSpec((None, 1, POOL_GC), lambda l, r: (l, 0, jnp.minimum(r, groups - 1))),
            blk,
        ],
        out_specs=blk,
        out_shape=jax.ShapeDtypeStruct(w_out.shape, BF16),
        compiler_params=_params("arbitrary", "arbitrary"),
        name="fold",
    )(w_pool, s_pool, w_out)


def _window_sum(ext, w):
    n = ext.shape[0]
    back = lambda a, k: pltpu.roll(a, n - k, 0)
    a, span = ext, 1
    while 2 * span < w:
        a = a + back(a, span)
        span *= 2
    half = w // 2
    tot = pltpu.roll(a, half, 0) + a
    return tot[HALO:n - HALO]


def _post_kernel(*refs, rows, tile, n_x, n_att, final):
    x_refs, att_refs, refs = refs[:n_x], refs[n_x:n_x + n_att], refs[n_x + n_att:]
    (u_ref, up_ref, un_ref, mod_ref, wm_ref, g_ref, w1_ref, w2_ref, *gf_ref,
     o_ref, y_ref) = refs
    lat_tiles = rows.lat // tile
    r0 = pl.program_id(0) * tile
    in_ctx = r0 >= rows.lat
    seg_len = jnp.where(in_ctx, rows.n_ctx, rows.n_lat)
    seg_base = jnp.where(in_ctx, rows.lat, 0)
    pieces = tile // POOL_PIECE
    for j in range(pieces):
        lo, hi = j * POOL_PIECE, (j + 1) * POOL_PIECE
        pos = lax.rem(r0 + lo - seg_base, seg_len)
        before = up_ref[...] if j == 0 else u_ref[lo - HALO:lo, :]
        after = un_ref[...] if j == pieces - 1 else u_ref[hi:hi + HALO, :]
        before = jnp.where(pos > 0, before, 0.0)
        after = jnp.where(pos + POOL_PIECE < seg_len, after, 0.0)
        cur = u_ref[lo:hi, :]
        ext = jnp.concatenate([before, cur, after], axis=0)
        trow = pos + lax.broadcasted_iota(jnp.int32, (POOL_PIECE, 1), 0)
        for gi, w in enumerate(POOL_WINDOWS):
            cols = slice(gi * POOL_GC, (gi + 1) * POOL_GC)
            cnt = jnp.minimum(trow + (w - w // 2), seg_len) - jnp.maximum(trow - w // 2, 0)
            y = _window_sum(ext[:, cols], w) / cnt.astype(F32) - cur[:, cols]
            y_ref[lo:hi, cols] = y.astype(BF16)
    mix = jnp.dot(y_ref[...], wm_ref[:POOL_WIDTH, :], preferred_element_type=F32)
    mix += jnp.dot(_stream_rows(att_refs, lat_tiles), wm_ref[POOL_WIDTH:, :],
                   preferred_element_type=F32)
    x = _stream_rows(x_refs, lat_tiles) + mod_ref[2:3, :] * mix

    h = _norm_mod(x, g_ref[...], mod_ref[3:4, :], mod_ref[4:5, :]).astype(BF16)
    acc = jnp.zeros(x.shape, F32)
    for c in range(D_FF // FF_CHUNK):
        cols = slice(c * FF_CHUNK, (c + 1) * FF_CHUNK)
        a = jnp.dot(h, w1_ref[:, cols], preferred_element_type=F32)
        a = jnp.square(jnp.maximum(a, 0.0)).astype(BF16)
        acc += jnp.dot(a, w2_ref[cols, :], preferred_element_type=F32)
    x = x + mod_ref[5:6, :] * acc
    if final:
        ms = jnp.mean(x * x, axis=-1, keepdims=True)
        x = x * lax.rsqrt(ms + EPS) * gf_ref[0][...]
    o_ref[...] = x


def _post_call(streams, att_streams, u, rows, layer, mods, w_mix, g, w1, w2, g_final=None):
    tile = POST_TILE
    final = g_final is not None
    n_rows = rows.lat if final else rows.all
    halo_per_tile = tile // HALO
    last_halo = rows.all // HALO - 1
    full = lambda width: pl.BlockSpec((tile, width), lambda t: (t, 0))
    whole = lambda w: pl.BlockSpec(w.shape, lambda t: (0,) * w.ndim, pipeline_mode=pl.Buffered(1))
    extra, extra_specs = [], []
    if final:
        extra, extra_specs = [g_final], [pl.BlockSpec((1, D_MODEL), lambda t: (0, 0))]
    return pl.pallas_call(
        functools.partial(_post_kernel, rows=rows, tile=tile, n_x=len(streams),
                          n_att=len(att_streams), final=final),
        grid=(n_rows // tile,),
        in_specs=rows.stream_specs(tile, D_MODEL, len(streams) == 2)
        + rows.stream_specs(tile, ATTN_WIDTH, len(att_streams) == 2) + [
            full(POOL_WIDTH),
            pl.BlockSpec((HALO, POOL_WIDTH), lambda t: (jnp.maximum(t * halo_per_tile - 1, 0), 0)),
            pl.BlockSpec((HALO, POOL_WIDTH),
                         lambda t: (jnp.minimum((t + 1) * halo_per_tile, last_halo), 0)),
            rows.mod_spec(layer, tile),
            pl.BlockSpec((None,) + w_mix.shape[1:], lambda t: (layer, 0, 0),
                         pipeline_mode=pl.Buffered(1)),
            _layer_spec(layer, 1, D_MODEL),
            whole(w1), whole(w2),
        ] + extra_specs,
        out_specs=full(D_MODEL),
        out_shape=jax.ShapeDtypeStruct((n_rows, D_MODEL), F32),
        scratch_shapes=[pltpu.VMEM((tile, POOL_WIDTH), BF16)],
        compiler_params=_params("arbitrary"),
        name="post",
    )(*streams, *att_streams, u, u, u, mods, w_mix, g, w1, w2, *extra)


def _rope_tables(n_lat, pad_rows):
    grid_rows = n_lat // GRID_W
    half = HEAD_DIM // 2
    row = jnp.repeat(jnp.arange(grid_rows), GRID_W).astype(F32)
    col = jnp.tile(jnp.arange(GRID_W), grid_rows).astype(F32)
    inv = ROPE_BASE ** (-jnp.arange(0, half, 2, dtype=F32) / half)
    ang = jnp.concatenate([row[:, None] * inv, col[:, None] * inv], axis=-1)
    cos, sin = jnp.cos(ang), jnp.sin(ang)
    reps = LANES // HEAD_DIM
    cos_t = jnp.tile(jnp.concatenate([cos, cos], axis=-1), (1, reps))
    sin_t = jnp.tile(jnp.concatenate([-sin, sin], axis=-1), (1, reps))
    cos_t = jnp.concatenate([cos_t, jnp.ones((pad_rows, LANES), F32)], axis=0)
    sin_t = jnp.concatenate([sin_t, jnp.zeros((pad_rows, LANES), F32)], axis=0)
    return cos_t, sin_t


def kernel(x, c, ctx, c_ctx, w_ada, b_ada, g_mix, g_mlp, w_in, w_pool, s_pool,
           lam_q1, lam_k1, lam_q2, lam_k2, g_subln, w_out, w_mlp1, w_mlp2, g_final):
    batch, n_lat, d_model = x.shape
    n_ctx = ctx.shape[1]
    depth = w_ada.shape[0]
    rows = _Rows(batch, n_lat, n_ctx)
    assert d_model == D_MODEL and batch < ADA_ROWS
    assert n_lat % ROW_TILE == 0 and rows.ctx % ROW_TILE == 0 and n_lat % n_ctx == 0
    assert n_lat % KV_CHUNK == 0 and n_lat % GRID_W == 0 and n_lat % Q_TILE == 0
    assert n_ctx % POOL_PIECE == 0 and ROW_TILE % POST_TILE == 0 and POST_TILE % POOL_PIECE == 0

    cvec = jnp.zeros((ADA_ROWS, D_MODEL), F32).at[:batch].set(c).at[batch].set(c_ctx)
    lam_init = [0.8 - 0.6 * math.exp(-0.3 * l) for l in range(depth)]
    linit = jnp.broadcast_to(jnp.asarray(lam_init, F32)[:, None, None], (depth, 1, LANES))
    vec3 = lambda a: a.reshape(depth, 1, a.shape[-1])
    mods, lam = _ada_call(cvec, w_ada, vec3(b_ada), vec3(lam_q1), vec3(lam_k1),
                          vec3(lam_q2), vec3(lam_k2), linit)
    mods = mods.reshape(depth, ADA_ROWS, N_ADA, D_MODEL)
    scal = jnp.stack([lam[:, 0, 0], 1.0 - jnp.asarray(lam_init, F32)], axis=-1)

    cos, sin = _rope_tables(n_lat, ROW_TILE)
    g_mix3, g_mlp3, g_sub3 = vec3(g_mix), vec3(g_mlp), vec3(g_subln)
    w_mix_b = _fold_call(w_pool, vec3(s_pool), w_out)

    streams = [x.reshape(rows.lat, D_MODEL), ctx.reshape(rows.ctx, D_MODEL)]
    w_in_b = w_in[0].astype(BF16)
    for l in range(depth):
        with_ctx = l < depth - 1
        u, q, k, v = _inproj_call(streams, rows, l, mods, g_mix3, w_in_b, cos, sin)
        casts = [(w_mlp1, l), (w_mlp2, l)] + ([(w_in, l + 1)] if with_ctx else [])
        att, att_ctx, (w1_b, w2_b, *w_in_next) = _attn_call(
            scal, q, k, v, g_sub3, rows, l, with_ctx, casts)
        xs = _post_call(streams, [att, att_ctx] if with_ctx else [att], u, rows, l, mods, w_mix_b,
                        g_mlp3, w1_b, w2_b, None if with_ctx else g_final[None])
        streams = [xs]
        if with_ctx:
            w_in_b = w_in_next[0]
    return xs.reshape(batch, n_lat, D_MODEL)
```

```python
import functools
import math

import jax
import jax.numpy as jnp
from jax import lax
from jax.experimental import pallas as pl
from jax.experimental.pallas import tpu as pltpu

D_MODEL = 1024
GRID_W = 64
POOL_WIDTH = 512
POOL_WINDOWS = (2, 4, 8, 16)
POOL_GC = 128
N_HEADS = 4
HEAD_DIM = 64
V_DIM = 128
QK_WIDTH = 512
ATTN_WIDTH = 512
IN_WIDTH = 2048
ROPE_BASE = 10000.0
D_FF = 4096
N_ADA = 6
EPS = 1e-6

LANES = 128
ADA_ROWS = 8
ROW_TILE = 1024
POST_TILE = 512
Q_TILE = 2048
POOL_PIECE = 256
HALO = 16
KV_CHUNK = 256
FF_CHUNK = 1024
ADA_COLS = 1024
VMEM_LIMIT = 56 * 1024 * 1024

F32 = jnp.float32
BF16 = jnp.bfloat16


def _params(*sem):
    return pltpu.CompilerParams(dimension_semantics=sem, vmem_limit_bytes=VMEM_LIMIT)


class _Rows:
    def __init__(self, batch, n_lat, n_ctx):
        self.batch, self.n_lat, self.n_ctx = batch, n_lat, n_ctx
        self.lat = batch * n_lat
        self.ctx = batch * n_ctx
        self.all = self.lat + self.ctx

    def mod_spec(self, layer, tile):
        lat_tiles, per_batch = self.lat // tile, self.n_lat // tile
        return pl.BlockSpec(
            (None, None, N_ADA, D_MODEL),
            lambda t: (layer, jnp.where(t < lat_tiles, t // per_batch, self.batch), 0, 0))

    def stream_specs(self, tile, width, split):
        if not split:
            return [pl.BlockSpec((tile, width), lambda t: (t, 0))]
        lat_tiles = self.lat // tile
        return [pl.BlockSpec((tile, width), lambda t: (jnp.minimum(t, lat_tiles - 1), 0)),
                pl.BlockSpec((tile, width), lambda t: (jnp.maximum(t - lat_tiles, 0), 0))]


def _stream_rows(refs, lat_tiles):
    if len(refs) == 1:
        return refs[0][...]
    return jnp.where(pl.program_id(0) >= lat_tiles, refs[1][...], refs[0][...])


def _layer_spec(layer, *shape):
    return pl.BlockSpec((None,) + shape, lambda t: (layer,) + (0,) * len(shape))


def _norm_mod(x, g, shift, scale):
    ms = jnp.mean(x * x, axis=-1, keepdims=True)
    return (x * lax.rsqrt(ms + EPS) * g) * (1.0 + scale) + shift


def _ada_kernel(c_ref, w_ref, b_ref, lq1_ref, lk1_ref, lq2_ref, lk2_ref, linit_ref,
                mod_ref, lam_ref):
    c = c_ref[...]
    s = c / (1.0 + jnp.exp(-c))
    mod_ref[0] = jnp.dot(s, w_ref[0], preferred_element_type=F32,
                         precision=lax.Precision.HIGHEST) + b_ref[0]
    d1 = jnp.sum(lq1_ref[0] * lk1_ref[0], axis=-1, keepdims=True)
    d2 = jnp.sum(lq2_ref[0] * lk2_ref[0], axis=-1, keepdims=True)
    lam = jnp.exp(d1) - jnp.exp(d2) + linit_ref[0]
    lam_ref[0] = jnp.broadcast_to(lam, (ADA_ROWS, LANES))


def _ada_call(cvec, w_ada, b_ada, lq1, lk1, lq2, lk2, linit):
    depth = w_ada.shape[0]
    width = w_ada.shape[2]
    vec = lambda: pl.BlockSpec((1, 1, HEAD_DIM), lambda l, j: (l, 0, 0))
    return pl.pallas_call(
        _ada_kernel,
        grid=(depth, width // ADA_COLS),
        in_specs=[
            pl.BlockSpec((ADA_ROWS, D_MODEL), lambda l, j: (0, 0)),
            pl.BlockSpec((1, D_MODEL, ADA_COLS), lambda l, j: (l, 0, j)),
            pl.BlockSpec((1, 1, ADA_COLS), lambda l, j: (l, 0, j)),
            vec(), vec(), vec(), vec(),
            pl.BlockSpec((1, 1, LANES), lambda l, j: (l, 0, 0)),
        ],
        out_specs=[
            pl.BlockSpec((1, ADA_ROWS, ADA_COLS), lambda l, j: (l, 0, j)),
            pl.BlockSpec((1, ADA_ROWS, LANES), lambda l, j: (l, 0, 0)),
        ],
        out_shape=[
            jax.ShapeDtypeStruct((depth, ADA_ROWS, width), F32),
            jax.ShapeDtypeStruct((depth, ADA_ROWS, LANES), F32),
        ],
        compiler_params=_params("arbitrary", "arbitrary"),
        name="ada",
    )(cvec, w_ada, b_ada, lq1, lk1, lq2, lk2, linit)


def _inproj_kernel(*refs, lat_tiles):
    *x_refs, mod_ref, g_ref, w_ref, cos_ref, sin_ref, u_ref, q_ref, k_ref, v_ref = refs
    x = _stream_rows(x_refs, lat_tiles)
    h = _norm_mod(x, g_ref[...], mod_ref[0:1, :], mod_ref[1:2, :]).astype(BF16)
    cos = cos_ref[...]
    sin = sin_ref[...]
    lane = lax.broadcasted_iota(jnp.int32, cos.shape, 1)
    first_half = (lane & (HEAD_DIM // 2)) == 0
    q_scale = HEAD_DIM ** -0.5 * math.log2(math.e)

    def rope(t):
        partner = jnp.where(first_half,
                            pltpu.roll(t, LANES - HEAD_DIM // 2, 1),
                            pltpu.roll(t, HEAD_DIM // 2, 1))
        return t * cos + partner * sin

    def proj(lo, width):
        return jnp.dot(h, w_ref[:, lo:lo + width], preferred_element_type=F32)

    u_ref[...] = proj(0, POOL_WIDTH)
    z = proj(POOL_WIDTH, QK_WIDTH)
    for c in range(QK_WIDTH // LANES):
        cols = slice(c * LANES, (c + 1) * LANES)
        q_ref[:, cols] = (rope(z[:, cols]) * q_scale).astype(BF16)
    z = proj(POOL_WIDTH + QK_WIDTH, QK_WIDTH)
    for c in range(QK_WIDTH // LANES):
        cols = slice(c * LANES, (c + 1) * LANES)
        k_ref[:, cols] = rope(z[:, cols]).astype(BF16)
    z = proj(POOL_WIDTH + 2 * QK_WIDTH, ATTN_WIDTH)
    ones = jnp.ones((z.shape[0], V_DIM), BF16)
    for hd in range(N_HEADS):
        v_ref[:, 2 * hd * V_DIM:(2 * hd + 1) * V_DIM] = z[:, hd * V_DIM:(hd + 1) * V_DIM].astype(BF16)
        v_ref[:, (2 * hd + 1) * V_DIM:(2 * hd + 2) * V_DIM] = ones


def _inproj_call(streams, rows, layer, mods, g, w_in, cos, sin):
    tile = ROW_TILE
    lat_tiles, per_batch = rows.lat // tile, rows.n_lat // tile
    split = len(streams) == 2
    table = pl.BlockSpec((tile, LANES),
                         lambda t: (jnp.where(t < lat_tiles, t % per_batch, per_batch), 0))
    out = lambda width: pl.BlockSpec((tile, width), lambda t: (t, 0))
    shape = lambda width, dt: jax.ShapeDtypeStruct((rows.all, width), dt)
    return pl.pallas_call(
        functools.partial(_inproj_kernel, lat_tiles=lat_tiles),
        grid=(rows.all // tile,),
        in_specs=rows.stream_specs(tile, D_MODEL, split) + [
            rows.mod_spec(layer, tile),
            _layer_spec(layer, 1, D_MODEL),
            pl.BlockSpec(w_in.shape, lambda t: (0, 0)),
            table, table,
        ],
        out_specs=[out(POOL_WIDTH), out(QK_WIDTH), out(QK_WIDTH), out(2 * ATTN_WIDTH)],
        out_shape=[shape(POOL_WIDTH, F32), shape(QK_WIDTH, BF16), shape(QK_WIDTH, BF16),
                   shape(2 * ATTN_WIDTH, BF16)],
        compiler_params=_params("arbitrary"),
        name="inproj",
    )(*streams, mods, g, w_in, cos, sin)


def _diff_attention(q, key_value_blocks, lam, gain):
    tq = q.shape[0]
    lane = lax.broadcasted_iota(jnp.int32, q.shape, 1)
    zero = jnp.zeros_like(q)
    qq = jnp.concatenate([jnp.where(lane < HEAD_DIM, q, zero),
                          jnp.where(lane >= HEAD_DIM, q, zero)], axis=0)

    def step(carry, kc, vc):
        m, acc = carry
        s = lax.dot_general(qq, kc, (((1,), (1,)), ((), ())), preferred_element_type=F32)
        m_new = jnp.maximum(m, jnp.max(s, axis=-1, keepdims=True))
        p = jnp.exp2(s - m_new).astype(BF16)
        pv = jnp.dot(p, vc, preferred_element_type=F32)
        if acc is None:
            return m_new, pv
        return m_new, jnp.exp2(m - m_new) * acc + pv

    carry = (jnp.full((2 * tq, 1), -jnp.inf, F32), None)
    for block in key_value_blocks:
        carry = step(carry, *block())
    acc = carry[1]
    o = acc[:, :V_DIM] / acc[:, V_DIM:]
    o = o[:tq] - lam * o[tq:]
    ms = jnp.mean(o * o, axis=-1, keepdims=True)
    return o * lax.rsqrt(ms + EPS) * gain


def _attn_kernel(scal_ref, q_ref, kl_ref, vl_ref, kc_ref, vc_ref, g_ref, *rest, layer):
    n_cast = len(rest) // 2
    cast_src, o_ref, cast_dst = rest[:n_cast], rest[n_cast], rest[n_cast + 1:]
    blocks = [functools.partial(lambda rows: (kl_ref[rows, :], vl_ref[rows, :]),
                                slice(c * KV_CHUNK, (c + 1) * KV_CHUNK))
              for c in range(kl_ref.shape[0] // KV_CHUNK)]
    blocks.append(lambda: (kc_ref[...], vc_ref[...]))
    o_ref[...] = _diff_attention(q_ref[...], blocks, scal_ref[layer, 0],
                                 g_ref[...] * scal_ref[layer, 1]).astype(o_ref.dtype)
    for src, dst in zip(cast_src, cast_dst):
        dst[...] = src[...].astype(dst.dtype)


def _attn_ctx_kernel(scal_ref, q_ref, k_ref, v_ref, g_ref, o_ref, *, layer):
    gain = g_ref[...] * scal_ref[layer, 1]
    for hd in range(N_HEADS):
        cols = slice(hd * V_DIM, (hd + 1) * V_DIM)
        vcols = slice(2 * hd * V_DIM, 2 * (hd + 1) * V_DIM)
        block = functools.partial(lambda c, vc: (k_ref[:, c], v_ref[:, vc]), cols, vcols)
        o_ref[:, cols] = _diff_attention(q_ref[:, cols], [block], scal_ref[layer, 0],
                                         gain).astype(o_ref.dtype)


def _attn_call(scal, q, k, v, g, rows, layer, with_ctx, casts):
    ctx_blocks = rows.lat // rows.n_ctx
    lat_tiles = rows.n_lat // Q_TILE
    steps = rows.batch * N_HEADS * lat_tiles
    smem = pl.BlockSpec(memory_space=pltpu.SMEM)
    lat = lambda width: pl.BlockSpec((rows.n_lat, width), lambda b, h, i: (b, h))
    ctx = lambda width: pl.BlockSpec((rows.n_ctx, width), lambda b, h, i: (ctx_blocks + b, h))
    gain = pl.BlockSpec((None, 1, V_DIM), lambda *_: (layer, 0, 0))
    q_lat = pl.BlockSpec((Q_TILE, V_DIM), lambda b, h, i: (b * lat_tiles + i, h))
    step = lambda b, h, i: (b * N_HEADS + h) * lat_tiles + i
    slab_in = lambda w, l: pl.BlockSpec((None, w.shape[1] // steps, w.shape[2]),
                                        lambda b, h, i: (l, step(b, h, i), 0))
    slab_out = lambda w: pl.BlockSpec((w.shape[1] // steps, w.shape[2]),
                                      lambda b, h, i: (step(b, h, i), 0))
    att, *cast_out = pl.pallas_call(
        functools.partial(_attn_kernel, layer=layer),
        grid=(rows.batch, N_HEADS, lat_tiles),
        in_specs=[smem, q_lat, lat(V_DIM), lat(2 * V_DIM), ctx(V_DIM), ctx(2 * V_DIM), gain]
        + [slab_in(w, l) for w, l in casts],
        out_specs=[q_lat] + [slab_out(w) for w, _ in casts],
        out_shape=[jax.ShapeDtypeStruct((rows.lat, ATTN_WIDTH), BF16)]
        + [jax.ShapeDtypeStruct(w.shape[1:], BF16) for w, _ in casts],
        compiler_params=_params("arbitrary", "arbitrary", "arbitrary"),
        name="attn",
    )(scal, q, k, v, k, v, g, *[w for w, _ in casts])
    if not with_ctx:
        return att, None, cast_out
    att_ctx = pl.pallas_call(
        functools.partial(_attn_ctx_kernel, layer=layer),
        grid=(rows.batch,),
        in_specs=[smem] + [pl.BlockSpec((rows.n_ctx, width), lambda b: (ctx_blocks + b, 0))
                           for width in (QK_WIDTH, QK_WIDTH, 2 * ATTN_WIDTH)] + [gain],
        out_specs=pl.BlockSpec((rows.n_ctx, ATTN_WIDTH), lambda b: (b, 0)),
        out_shape=jax.ShapeDtypeStruct((rows.ctx, ATTN_WIDTH), BF16),
        compiler_params=_params("arbitrary"),
        name="attn_ctx",
    )(scal, q, k, v, g)
    return att, att_ctx, cast_out


def _fold_kernel(wp_ref, sp_ref, wo_ref, o_ref):
    r = pl.program_id(1)

    @pl.when(r < POOL_WIDTH // POOL_GC)
    def _():
        o_ref[...] = jnp.dot(wp_ref[...] * sp_ref[...], wo_ref[...], preferred_element_type=F32,
                             precision=lax.Precision.HIGHEST).astype(o_ref.dtype)

    @pl.when(r >= POOL_WIDTH // POOL_GC)
    def _():
        o_ref[...] = wo_ref[...].astype(o_ref.dtype)


def _fold_call(w_pool, s_pool, w_out):
    depth, mix_width, d_model = w_out.shape
    groups = POOL_WIDTH // POOL_GC
    blk = pl.BlockSpec((None, POOL_GC, d_model), lambda l, r: (l, r, 0))
    return pl.pallas_call(
        _fold_kernel,
        grid=(depth, mix_width // POOL_GC),
        in_specs=[
            pl.BlockSpec((None, None, POOL_GC, POOL_GC),
                         lambda l, r: (l, jnp.minimum(r, groups - 1), 0, 0)),
            pl.BlockSpec((None, 1, POOL_GC), lambda l, r: (l, 0, jnp.minimum(r, groups - 1))),
            blk,
        ],
        out_specs=blk,
        out_shape=jax.ShapeDtypeStruct(w_out.shape, BF16),
        compiler_params=_params("arbitrary", "arbitrary"),
        name="fold",
    )(w_pool, s_pool, w_out)


def _window_sum(ext, w):
    n = ext.shape[0]
    back = lambda a, k: pltpu.roll(a, n - k, 0)
    a, span = ext, 1
    while 2 * span < w:
        a = a + back(a, span)
        span *= 2
    half = w // 2
    tot = pltpu.roll(a, half, 0) + a
    return tot[HALO:n - HALO]


def _post_kernel(*refs, rows, tile, n_x, n_att, final):
    x_refs, att_refs, refs = refs[:n_x], refs[n_x:n_x + n_att], refs[n_x + n_att:]
    (u_ref, up_ref, un_ref, mod_ref, wm_ref, g_ref, w1_ref, w2_ref, *gf_ref,
     o_ref, y_ref) = refs
    lat_tiles = rows.lat // tile
    r0 = pl.program_id(0) * tile
    in_ctx = r0 >= rows.lat
    seg_len = jnp.where(in_ctx, rows.n_ctx, rows.n_lat)
    seg_base = jnp.where(in_ctx, rows.lat, 0)
    pieces = tile // POOL_PIECE
    for j in range(pieces):
        lo, hi = j * POOL_PIECE, (j + 1) * POOL_PIECE
        pos = lax.rem(r0 + lo - seg_base, seg_len)
        before = up_ref[...] if j == 0 else u_ref[lo - HALO:lo, :]
        after = un_ref[...] if j == pieces - 1 else u_ref[hi:hi + HALO, :]
        before = jnp.where(pos > 0, before, 0.0)
        after = jnp.where(pos + POOL_PIECE < seg_len, after, 0.0)
        cur = u_ref[lo:hi, :]
        ext = jnp.concatenate([before, cur, after], axis=0)
        trow = pos + lax.broadcasted_iota(jnp.int32, (POOL_PIECE, 1), 0)
        for gi, w in enumerate(POOL_WINDOWS):
            cols = slice(gi * POOL_GC, (gi + 1) * POOL_GC)
            cnt = jnp.minimum(trow + (w - w // 2), seg_len) - jnp.maximum(trow - w // 2, 0)
            y = _window_sum(ext[:, cols], w) / cnt.astype(F32) - cur[:, cols]
            y_ref[lo:hi, cols] = y.astype(BF16)
    mix = jnp.dot(y_ref[...], wm_ref[:POOL_WIDTH, :], preferred_element_type=F32)
    mix += jnp.dot(_stream_rows(att_refs, lat_tiles), wm_ref[POOL_WIDTH:, :],
                   preferred_element_type=F32)
    x = _stream_rows(x_refs, lat_tiles) + mod_ref[2:3, :] * mix

    h = _norm_mod(x, g_ref[...], mod_ref[3:4, :], mod_ref[4:5, :]).astype(BF16)
    acc = jnp.zeros(x.shape, F32)
    for c in range(D_FF // FF_CHUNK):
        cols = slice(c * FF_CHUNK, (c + 1) * FF_CHUNK)
        a = jnp.dot(h, w1_ref[:, cols], preferred_element_type=F32)
        a = jnp.square(jnp.maximum(a, 0.0)).astype(BF16)
        acc += jnp.dot(a, w2_ref[cols, :], preferred_element_type=F32)
    x = x + mod_ref[5:6, :] * acc
    if final:
        ms = jnp.mean(x * x, axis=-1, keepdims=True)
        x = x * lax.rsqrt(ms + EPS) * gf_ref[0][...]
    o_ref[...] = x


def _post_call(streams, att_streams, u, rows, layer, mods, w_mix, g, w1, w2, g_final=None):
    tile = POST_TILE
    final = g_final is not None
    n_rows = rows.lat if final else rows.all
    halo_per_tile = tile // HALO
    last_halo = rows.all // HALO - 1
    full = lambda width: pl.BlockSpec((tile, width), lambda t: (t, 0))
    whole = lambda w: pl.BlockSpec(w.shape, lambda t: (0,) * w.ndim, pipeline_mode=pl.Buffered(1))
    extra, extra_specs = [], []
    if final:
        extra, extra_specs = [g_final], [pl.BlockSpec((1, D_MODEL), lambda t: (0, 0))]
    return pl.pallas_call(
        functools.partial(_post_kernel, rows=rows, tile=tile, n_x=len(streams),
                          n_att=len(att_streams), final=final),
        grid=(n_rows // tile,),
        in_specs=rows.stream_specs(tile, D_MODEL, len(streams) == 2)
        + rows.stream_specs(tile, ATTN_WIDTH, len(att_streams) == 2) + [
            full(POOL_WIDTH),
            pl.BlockSpec((HALO, POOL_WIDTH), lambda t: (jnp.maximum(t * halo_per_tile - 1, 0), 0)),
            pl.BlockSpec((HALO, POOL_WIDTH),
                         lambda t: (jnp.minimum((t + 1) * halo_per_tile, last_halo), 0)),
            rows.mod_spec(layer, tile),
            pl.BlockSpec((None,) + w_mix.shape[1:], lambda t: (layer, 0, 0),
                         pipeline_mode=pl.Buffered(1)),
            _layer_spec(layer, 1, D_MODEL),
            whole(w1), whole(w2),
        ] + extra_specs,
        out_specs=full(D_MODEL),
        out_shape=jax.ShapeDtypeStruct((n_rows, D_MODEL), F32),
        scratch_shapes=[pltpu.VMEM((tile, POOL_WIDTH), BF16)],
        compiler_params=_params("arbitrary"),
        name="post",
    )(*streams, *att_streams, u, u, u, mods, w_mix, g, w1, w2, *extra)


def _rope_tables(n_lat, pad_rows):
    grid_rows = n_lat // GRID_W
    half = HEAD_DIM // 2
    row = jnp.repeat(jnp.arange(grid_rows), GRID_W).astype(F32)
    col = jnp.tile(jnp.arange(GRID_W), grid_rows).astype(F32)
    inv = ROPE_BASE ** (-jnp.arange(0, half, 2, dtype=F32) / half)
    ang = jnp.concatenate([row[:, None] * inv, col[:, None] * inv], axis=-1)
    cos, sin = jnp.cos(ang), jnp.sin(ang)
    reps = LANES // HEAD_DIM
    cos_t = jnp.tile(jnp.concatenate([cos, cos], axis=-1), (1, reps))
    sin_t = jnp.tile(jnp.concatenate([-sin, sin], axis=-1), (1, reps))
    cos_t = jnp.concatenate([cos_t, jnp.ones((pad_rows, LANES), F32)], axis=0)
    sin_t = jnp.concatenate([sin_t, jnp.zeros((pad_rows, LANES), F32)], axis=0)
    return cos_t, sin_t


def kernel(x, c, ctx, c_ctx, w_ada, b_ada, g_mix, g_mlp, w_in, w_pool, s_pool,
           lam_q1, lam_k1, lam_q2, lam_k2, g_subln, w_out, w_mlp1, w_mlp2, g_final):
    batch, n_lat, d_model = x.shape
    n_ctx = ctx.shape[1]
    depth = w_ada.shape[0]
    rows = _Rows(batch, n_lat, n_ctx)
    assert d_model == D_MODEL and batch < ADA_ROWS
    assert n_lat % ROW_TILE == 0 and rows.ctx % ROW_TILE == 0 and n_lat % n_ctx == 0
    assert n_lat % KV_CHUNK == 0 and n_lat % GRID_W == 0 and n_lat % Q_TILE == 0
    assert n_ctx % POOL_PIECE == 0 and ROW_TILE % POST_TILE == 0 and POST_TILE % POOL_PIECE == 0

    cvec = jnp.zeros((ADA_ROWS, D_MODEL), F32).at[:batch].set(c).at[batch].set(c_ctx)
    lam_init = [0.8 - 0.6 * math.exp(-0.3 * l) for l in range(depth)]
    linit = jnp.broadcast_to(jnp.asarray(lam_init, F32)[:, None, None], (depth, 1, LANES))
    vec3 = lambda a: a.reshape(depth, 1, a.shape[-1])
    mods, lam = _ada_call(cvec, w_ada, vec3(b_ada), vec3(lam_q1), vec3(lam_k1),
                          vec3(lam_q2), vec3(lam_k2), linit)
    mods = mods.reshape(depth, ADA_ROWS, N_ADA, D_MODEL)
    scal = jnp.stack([lam[:, 0, 0], 1.0 - jnp.asarray(lam_init, F32)], axis=-1)

    cos, sin = _rope_tables(n_lat, ROW_TILE)
    g_mix3, g_mlp3, g_sub3 = vec3(g_mix), vec3(g_mlp), vec3(g_subln)
    w_mix_b = _fold_call(w_pool, vec3(s_pool), w_out)

    streams = [x.reshape(rows.lat, D_MODEL), ctx.reshape(rows.ctx, D_MODEL)]
    w_in_b = w_in[0].astype(BF16)
    for l in range(depth):
        with_ctx = l < depth - 1
        u, q, k, v = _inproj_call(streams, rows, l, mods, g_mix3, w_in_b, cos, sin)
        casts = [(w_mlp1, l), (w_mlp2, l)] + ([(w_in, l + 1)] if with_ctx else [])
        att, att_ctx, (w1_b, w2_b, *w_in_next) = _attn_call(
            scal, q, k, v, g_sub3, rows, l, with_ctx, casts)
        xs = _post_call(streams, [att, att_ctx] if with_ctx else [att], u, rows, l, mods, w_mix_b,
                        g_mlp3, w1_b, w2_b, None if with_ctx else g_final[None])
        streams = [xs]
        if with_ctx:
            w_in_b = w_in_next[0]
    return xs.reshape(batch, n_lat, D_MODEL)
```

```python
import functools
import math

import jax
import jax.numpy as jnp
from jax import lax
from jax.experimental import pallas as pl
from jax.experimental.pallas import tpu as pltpu

D_MODEL = 1024
GRID_W = 64
POOL_WIDTH = 512
POOL_WINDOWS = (2, 4, 8, 16)
POOL_GC = 128
N_HEADS = 4
HEAD_DIM = 64
V_DIM = 128
QK_WIDTH = 512
ATTN_WIDTH = 512
IN_WIDTH = 2048
ROPE_BASE = 10000.0
D_FF = 4096
N_ADA = 6
EPS = 1e-6

LANES = 128
ADA_ROWS = 8
ROW_TILE = 1024
POST_TILE = 512
Q_TILE = 2048
POOL_PIECE = 256
HALO = 16
KV_CHUNK = 256
FF_CHUNK = 1024
ADA_COLS = 1024
VMEM_LIMIT = 56 * 1024 * 1024

F32 = jnp.float32
BF16 = jnp.bfloat16


def _params(*sem):
    return pltpu.CompilerParams(dimension_semantics=sem, vmem_limit_bytes=VMEM_LIMIT)


class _Rows:
    def __init__(self, batch, n_lat, n_ctx):
        self.batch, self.n_lat, self.n_ctx = batch, n_lat, n_ctx
        self.lat = batch * n_lat
        self.ctx = batch * n_ctx
        self.all = self.lat + self.ctx

    def mod_spec(self, layer, tile):
        lat_tiles, per_batch = self.lat // tile, self.n_lat // tile
        return pl.BlockSpec(
            (None, None, N_ADA, D_MODEL),
            lambda t: (layer, jnp.where(t < lat_tiles, t // per_batch, self.batch), 0, 0))

    def stream_specs(self, tile, width, split):
        if not split:
            return [pl.BlockSpec((tile, width), lambda t: (t, 0))]
        lat_tiles = self.lat // tile
        return [pl.BlockSpec((tile, width), lambda t: (jnp.minimum(t, lat_tiles - 1), 0)),
                pl.BlockSpec((tile, width), lambda t: (jnp.maximum(t - lat_tiles, 0), 0))]


def _stream_rows(refs, lat_tiles):
    if len(refs) == 1:
        return refs[0][...]
    return jnp.where(pl.program_id(0) >= lat_tiles, refs[1][...], refs[0][...])


def _layer_spec(layer, *shape):
    return pl.BlockSpec((None,) + shape, lambda t: (layer,) + (0,) * len(shape))


def _norm_mod(x, g, shift, scale):
    ms = jnp.mean(x * x, axis=-1, keepdims=True)
    return (x * lax.rsqrt(ms + EPS) * g) * (1.0 + scale) + shift


def _split_dot(a, b):
    a_hi = a.astype(BF16)
    a_lo = (a - a_hi.astype(F32)).astype(BF16)
    b_hi = b.astype(BF16)
    b_lo = (b - b_hi.astype(F32)).astype(BF16)
    rows = a.shape[0]
    both = jnp.dot(jnp.concatenate([a_hi, a_lo], axis=0), b_hi, preferred_element_type=F32)
    return both[:rows] + both[rows:] + jnp.dot(a_hi, b_lo, preferred_element_type=F32)


def _ada_kernel(c_ref, w_ref, b_ref, lq1_ref, lk1_ref, lq2_ref, lk2_ref, linit_ref,
                mod_ref, lam_ref):
    c = c_ref[...]
    s = c / (1.0 + jnp.exp(-c))
    mod_ref[0] = _split_dot(s, w_ref[0]) + b_ref[0]
    d1 = jnp.sum(lq1_ref[0] * lk1_ref[0], axis=-1, keepdims=True)
    d2 = jnp.sum(lq2_ref[0] * lk2_ref[0], axis=-1, keepdims=True)
    lam = jnp.exp(d1) - jnp.exp(d2) + linit_ref[0]
    lam_ref[0] = jnp.broadcast_to(lam, (ADA_ROWS, LANES))


def _ada_call(cvec, w_ada, b_ada, lq1, lk1, lq2, lk2, linit):
    depth = w_ada.shape[0]
    width = w_ada.shape[2]
    vec = lambda: pl.BlockSpec((1, 1, HEAD_DIM), lambda l, j: (l, 0, 0))
    return pl.pallas_call(
        _ada_kernel,
        grid=(depth, width // ADA_COLS),
        in_specs=[
            pl.BlockSpec((ADA_ROWS, D_MODEL), lambda l, j: (0, 0)),
            pl.BlockSpec((1, D_MODEL, ADA_COLS), lambda l, j: (l, 0, j)),
            pl.BlockSpec((1, 1, ADA_COLS), lambda l, j: (l, 0, j)),
            vec(), vec(), vec(), vec(),
            pl.BlockSpec((1, 1, LANES), lambda l, j: (l, 0, 0)),
        ],
        out_specs=[
            pl.BlockSpec((1, ADA_ROWS, ADA_COLS), lambda l, j: (l, 0, j)),
            pl.BlockSpec((1, ADA_ROWS, LANES), lambda l, j: (l, 0, 0)),
        ],
        out_shape=[
            jax.ShapeDtypeStruct((depth, ADA_ROWS, width), F32),
            jax.ShapeDtypeStruct((depth, ADA_ROWS, LANES), F32),
        ],
        compiler_params=_params("arbitrary", "arbitrary"),
        name="ada",
    )(cvec, w_ada, b_ada, lq1, lk1, lq2, lk2, linit)


def _project_in(x, mod_ref, g_ref, w_ref, cos_ref, sin_ref, u_ref, q_ref, k_ref, v_ref):
    h = _norm_mod(x, g_ref[...], mod_ref[0:1, :], mod_ref[1:2, :]).astype(BF16)
    cos = cos_ref[...]
    sin = sin_ref[...]
    lane = lax.broadcasted_iota(jnp.int32, cos.shape, 1)
    first_half = (lane & (HEAD_DIM // 2)) == 0
    q_scale = HEAD_DIM ** -0.5 * math.log2(math.e)

    def rope(t):
        partner = jnp.where(first_half,
                            pltpu.roll(t, LANES - HEAD_DIM // 2, 1),
                            pltpu.roll(t, HEAD_DIM // 2, 1))
        return t * cos + partner * sin

    def proj(lo, width):
        return jnp.dot(h, w_ref[:, lo:lo + width], preferred_element_type=F32)

    u_ref[...] = proj(0, POOL_WIDTH)
    z = proj(POOL_WIDTH, QK_WIDTH)
    for c in range(QK_WIDTH // LANES):
        cols = slice(c * LANES, (c + 1) * LANES)
        q_ref[:, cols] = (rope(z[:, cols]) * q_scale).astype(BF16)
    z = proj(POOL_WIDTH + QK_WIDTH, QK_WIDTH)
    for c in range(QK_WIDTH // LANES):
        cols = slice(c * LANES, (c + 1) * LANES)
        k_ref[:, cols] = rope(z[:, cols]).astype(BF16)
    z = proj(POOL_WIDTH + 2 * QK_WIDTH, ATTN_WIDTH)
    ones = jnp.ones((z.shape[0], V_DIM), BF16)
    for hd in range(N_HEADS):
        v_ref[:, 2 * hd * V_DIM:(2 * hd + 1) * V_DIM] = z[:, hd * V_DIM:(hd + 1) * V_DIM].astype(BF16)
        v_ref[:, (2 * hd + 1) * V_DIM:(2 * hd + 2) * V_DIM] = ones


def _inproj_kernel(*refs, lat_tiles):
    *x_refs, mod_ref, g_ref, w_ref, cos_ref, sin_ref, u_ref, q_ref, k_ref, v_ref = refs
    _project_in(_stream_rows(x_refs, lat_tiles), mod_ref, g_ref, w_ref, cos_ref, sin_ref,
                u_ref, q_ref, k_ref, v_ref)


def _table_spec(rows, tile):
    lat_tiles, per_batch = rows.lat // tile, rows.n_lat // tile
    return pl.BlockSpec((tile, LANES),
                        lambda t: (jnp.where(t < lat_tiles, t % per_batch, per_batch), 0))


def _mixer_input_outs(rows, tile):
    out = lambda width: pl.BlockSpec((tile, width), lambda t: (t, 0))
    shape = lambda width, dt: jax.ShapeDtypeStruct((rows.all, width), dt)
    return ([out(POOL_WIDTH), out(QK_WIDTH), out(QK_WIDTH), out(2 * ATTN_WIDTH)],
            [shape(POOL_WIDTH, F32), shape(QK_WIDTH, BF16), shape(QK_WIDTH, BF16),
             shape(2 * ATTN_WIDTH, BF16)])


def _inproj_call(streams, rows, layer, mods, g, w_in, cos, sin):
    tile = ROW_TILE
    lat_tiles = rows.lat // tile
    split = len(streams) == 2
    table = _table_spec(rows, tile)
    out_specs, out_shape = _mixer_input_outs(rows, tile)
    return pl.pallas_call(
        functools.partial(_inproj_kernel, lat_tiles=lat_tiles),
        grid=(rows.all // tile,),
        in_specs=rows.stream_specs(tile, D_MODEL, split) + [
            rows.mod_spec(layer, tile),
            _layer_spec(layer, 1, D_MODEL),
            pl.BlockSpec(w_in.shape, lambda t: (0, 0)),
            table, table,
        ],
        out_specs=out_specs,
        out_shape=out_shape,
        compiler_params=_params("arbitrary"),
        name="inproj",
    )(*streams, mods, g, w_in, cos, sin)


def _diff_attention(q, key_value_blocks, lam, gain):
    tq = q.shape[0]
    lane = lax.broadcasted_iota(jnp.int32, q.shape, 1)
    zero = jnp.zeros_like(q)
    qq = jnp.concatenate([jnp.where(lane < HEAD_DIM, q, zero),
                          jnp.where(lane >= HEAD_DIM, q, zero)], axis=0)

    def step(carry, kc, vc):
        m, acc = carry
        s = lax.dot_general(qq, kc, (((1,), (1,)), ((), ())), preferred_element_type=F32)
        m_new = jnp.maximum(m, jnp.max(s, axis=-1, keepdims=True))
        p = jnp.exp2(s - m_new).astype(BF16)
        pv = jnp.dot(p, vc, preferred_element_type=F32)
        if acc is None:
            return m_new, pv
        return m_new, jnp.exp2(m - m_new) * acc + pv

    carry = (jnp.full((2 * tq, 1), -jnp.inf, F32), None)
    for block in key_value_blocks:
        carry = step(carry, *block())
    acc = carry[1]
    o = acc[:, :V_DIM] / acc[:, V_DIM:]
    o = o[:tq] - lam * o[tq:]
    ms = jnp.mean(o * o, axis=-1, keepdims=True)
    return o * lax.rsqrt(ms + EPS) * gain


def _attn_kernel(scal_ref, q_ref, kl_ref, vl_ref, kc_ref, vc_ref, g_ref, *rest, layer):
    n_cast = len(rest) // 2
    cast_src, o_ref, cast_dst = rest[:n_cast], rest[n_cast], rest[n_cast + 1:]
    blocks = [functools.partial(lambda rows: (kl_ref[rows, :], vl_ref[rows, :]),
                                slice(c * KV_CHUNK, (c + 1) * KV_CHUNK))
              for c in range(kl_ref.shape[0] // KV_CHUNK)]
    blocks.append(lambda: (kc_ref[...], vc_ref[...]))
    o_ref[...] = _diff_attention(q_ref[...], blocks, scal_ref[layer, 0],
                                 g_ref[...] * scal_ref[layer, 1]).astype(o_ref.dtype)
    for src, dst in zip(cast_src, cast_dst):
        dst[...] = src[...].astype(dst.dtype)


def _attn_ctx_kernel(scal_ref, q_ref, k_ref, v_ref, g_ref, o_ref, *, layer):
    gain = g_ref[...] * scal_ref[layer, 1]
    for hd in range(N_HEADS):
        cols = slice(hd * V_DIM, (hd + 1) * V_DIM)
        vcols = slice(2 * hd * V_DIM, 2 * (hd + 1) * V_DIM)
        block = functools.partial(lambda c, vc: (k_ref[:, c], v_ref[:, vc]), cols, vcols)
        o_ref[:, cols] = _diff_attention(q_ref[:, cols], [block], scal_ref[layer, 0],
                                         gain).astype(o_ref.dtype)


def _attn_call(scal, q, k, v, g, rows, layer, with_ctx, casts):
    ctx_blocks = rows.lat // rows.n_ctx
    lat_tiles = rows.n_lat // Q_TILE
    steps = rows.batch * N_HEADS * lat_tiles
    smem = pl.BlockSpec(memory_space=pltpu.SMEM)
    lat = lambda width: pl.BlockSpec((rows.n_lat, width), lambda b, h, i: (b, h))
    ctx = lambda width: pl.BlockSpec((rows.n_ctx, width), lambda b, h, i: (ctx_blocks + b, h))
    gain = pl.BlockSpec((None, 1, V_DIM), lambda *_: (layer, 0, 0))
    q_lat = pl.BlockSpec((Q_TILE, V_DIM), lambda b, h, i: (b * lat_tiles + i, h))
    step = lambda b, h, i: (b * N_HEADS + h) * lat_tiles + i
    slab_in = lambda w, l: pl.BlockSpec((None, w.shape[1] // steps, w.shape[2]),
                                        lambda b, h, i: (l, step(b, h, i), 0))
    slab_out = lambda w: pl.BlockSpec((w.shape[1] // steps, w.shape[2]),
                                      lambda b, h, i: (step(b, h, i), 0))
    att, *cast_out = pl.pallas_call(
        functools.partial(_attn_kernel, layer=layer),
        grid=(rows.batch, N_HEADS, lat_tiles),
        in_specs=[smem, q_lat, lat(V_DIM), lat(2 * V_DIM), ctx(V_DIM), ctx(2 * V_DIM), gain]
        + [slab_in(w, l) for w, l in casts],
        out_specs=[q_lat] + [slab_out(w) for w, _ in casts],
        out_shape=[jax.ShapeDtypeStruct((rows.lat, ATTN_WIDTH), BF16)]
        + [jax.ShapeDtypeStruct(w.shape[1:], BF16) for w, _ in casts],
        compiler_params=_params("arbitrary", "arbitrary", "arbitrary"),
        name="attn",
    )(scal, q, k, v, k, v, g, *[w for w, _ in casts])
    if not with_ctx:
        return att, None, cast_out
    att_ctx = pl.pallas_call(
        functools.partial(_attn_ctx_kernel, layer=layer),
        grid=(rows.batch,),
        in_specs=[smem] + [pl.BlockSpec((rows.n_ctx, width), lambda b: (ctx_blocks + b, 0))
                           for width in (QK_WIDTH, QK_WIDTH, 2 * ATTN_WIDTH)] + [gain],
        out_specs=pl.BlockSpec((rows.n_ctx, ATTN_WIDTH), lambda b: (b, 0)),
        out_shape=jax.ShapeDtypeStruct((rows.ctx, ATTN_WIDTH), BF16),
        compiler_params=_params("arbitrary"),
        name="attn_ctx",
    )(scal, q, k, v, g)
    return att, att_ctx, cast_out


def _fold_kernel(wp_ref, sp_ref, wo_ref, o_ref):
    r = pl.program_id(1)

    @pl.when(r < POOL_WIDTH // POOL_GC)
    def _():
        o_ref[...] = _split_dot(wp_ref[...] * sp_ref[...], wo_ref[...]).astype(o_ref.dtype)

    @pl.when(r >= POOL_WIDTH // POOL_GC)
    def _():
        o_ref[...] = wo_ref[...].astype(o_ref.dtype)


def _fold_call(w_pool, s_pool, w_out):
    depth, mix_width, d_model = w_out.shape
    groups = POOL_WIDTH // POOL_GC
    blk = pl.BlockSpec((None, POOL_GC, d_model), lambda l, r: (l, r, 0))
    return pl.pallas_call(
        _fold_kernel,
        grid=(depth, mix_width // POOL_GC),
        in_specs=[
            pl.BlockSpec((None, None, POOL_GC, POOL_GC),
                         lambda l, r: (l, jnp.minimum(r, groups - 1), 0, 0)),
            pl.BlockSpec((None, 1, POOL_GC), lambda l, r: (l, 0, jnp.minimum(r, groups - 1))),
            blk,
        ],
        out_specs=blk,
        out_shape=jax.ShapeDtypeStruct(w_out.shape, BF16),
        compiler_params=_params("arbitrary", "arbitrary"),
        name="fold",
    )(w_pool, s_pool, w_out)


def _window_sum(ext, w):
    n = ext.shape[0]
    back = lambda a, k: pltpu.roll(a, n - k, 0)
    a, span = ext, 1
    while 2 * span < w:
        a = a + back(a, span)
        span *= 2
    half = w // 2
    tot = pltpu.roll(a, half, 0) + a
    return tot[HALO:n - HALO]


def _post_kernel(*refs, rows, tile, n_x, n_att, final):
    x_refs, att_refs, refs = refs[:n_x], refs[n_x:n_x + n_att], refs[n_x + n_att:]
    u_ref, up_ref, un_ref, mod_ref, wm_ref, g_ref, w1_ref, w2_ref, *refs = refs
    if final:
        gf_ref, o_ref, y_ref = refs
    else:
        *next_in, o_ref, u2_ref, q2_ref, k2_ref, v2_ref, y_ref = refs
    lat_tiles = rows.lat // tile
    r0 = pl.program_id(0) * tile
    in_ctx = r0 >= rows.lat
    seg_len = jnp.where(in_ctx, rows.n_ctx, rows.n_lat)
    seg_base = jnp.where(in_ctx, rows.lat, 0)
    pieces = tile // POOL_PIECE
    for j in range(pieces):
        lo, hi = j * POOL_PIECE, (j + 1) * POOL_PIECE
        pos = lax.rem(r0 + lo - seg_base, seg_len)
        before = up_ref[...] if j == 0 else u_ref[lo - HALO:lo, :]
        after = un_ref[...] if j == pieces - 1 else u_ref[hi:hi + HALO, :]
        before = jnp.where(pos > 0, before, 0.0)
        after = jnp.where(pos + POOL_PIECE < seg_len, after, 0.0)
        cur = u_ref[lo:hi, :]
        ext = jnp.concatenate([before, cur, after], axis=0)
        trow = pos + lax.broadcasted_iota(jnp.int32, (POOL_PIECE, 1), 0)
        for gi, w in enumerate(POOL_WINDOWS):
            cols = slice(gi * POOL_GC, (gi + 1) * POOL_GC)
            cnt = jnp.minimum(trow + (w - w // 2), seg_len) - jnp.maximum(trow - w // 2, 0)
            y = _window_sum(ext[:, cols], w) / cnt.astype(F32) - cur[:, cols]
            y_ref[lo:hi, cols] = y.astype(BF16)
    mix = jnp.dot(y_ref[...], wm_ref[:POOL_WIDTH, :], preferred_element_type=F32)
    mix += jnp.dot(_stream_rows(att_refs, lat_tiles), wm_ref[POOL_WIDTH:, :],
                   preferred_element_type=F32)
    x = _stream_rows(x_refs, lat_tiles) + mod_ref[2:3, :] * mix

    h = _norm_mod(x, g_ref[...], mod_ref[3:4, :], mod_ref[4:5, :]).astype(BF16)
    acc = jnp.zeros(x.shape, F32)
    for c in range(D_FF // FF_CHUNK):
        cols = slice(c * FF_CHUNK, (c + 1) * FF_CHUNK)
        a = jnp.dot(h, w1_ref[:, cols], preferred_element_type=F32)
        a = jnp.square(jnp.maximum(a, 0.0)).astype(BF16)
        acc += jnp.dot(a, w2_ref[cols, :], preferred_element_type=F32)
    x = x + mod_ref[5:6, :] * acc
    if final:
        ms = jnp.mean(x * x, axis=-1, keepdims=True)
        x = x * lax.rsqrt(ms + EPS) * gf_ref[...]
    o_ref[...] = x
    if not final:
        _project_in(x, *next_in, u2_ref, q2_ref, k2_ref, v2_ref)


def _post_call(streams, att_streams, u, rows, layer, mods, w_mix, g, w1, w2, g_final=None,
               next_in=None):
    tile = POST_TILE
    final = g_final is not None
    n_rows = rows.lat if final else rows.all
    halo_per_tile = tile // HALO
    last_halo = rows.all // HALO - 1
    full = lambda width: pl.BlockSpec((tile, width), lambda t: (t, 0))
    whole = lambda w: pl.BlockSpec(w.shape, lambda t: (0,) * w.ndim, pipeline_mode=pl.Buffered(1))
    out_specs, out_shape = [full(D_MODEL)], [jax.ShapeDtypeStruct((n_rows, D_MODEL), F32)]
    if final:
        extra, extra_specs = [g_final], [pl.BlockSpec((1, D_MODEL), lambda t: (0, 0))]
    else:
        g_next, w_in, cos, sin = next_in
        table = _table_spec(rows, tile)
        extra = [mods, g_next, w_in, cos, sin]
        extra_specs = [rows.mod_spec(layer + 1, tile), _layer_spec(layer + 1, 1, D_MODEL),
                       whole(w_in), table, table]
        next_specs, next_shape = _mixer_input_outs(rows, tile)
        out_specs, out_shape = out_specs + next_specs, out_shape + next_shape
    return pl.pallas_call(
        functools.partial(_post_kernel, rows=rows, tile=tile, n_x=len(streams),
                          n_att=len(att_streams), final=final),
        grid=(n_rows // tile,),
        in_specs=rows.stream_specs(tile, D_MODEL, len(streams) == 2)
        + rows.stream_specs(tile, ATTN_WIDTH, len(att_streams) == 2) + [
            full(POOL_WIDTH),
            pl.BlockSpec((HALO, POOL_WIDTH), lambda t: (jnp.maximum(t * halo_per_tile - 1, 0), 0)),
            pl.BlockSpec((HALO, POOL_WIDTH),
                         lambda t: (jnp.minimum((t + 1) * halo_per_tile, last_halo), 0)),
            rows.mod_spec(layer, tile),
            pl.BlockSpec((None,) + w_mix.shape[1:], lambda t: (layer, 0, 0),
                         pipeline_mode=pl.Buffered(1)),
            _layer_spec(layer, 1, D_MODEL),
            whole(w1), whole(w2),
        ] + extra_specs,
        out_specs=out_specs,
        out_shape=out_shape,
        scratch_shapes=[pltpu.VMEM((tile, POOL_WIDTH), BF16)],
        compiler_params=_params("arbitrary"),
        name="post",
    )(*streams, *att_streams, u, u, u, mods, w_mix, g, w1, w2, *extra)


def _rope_tables(n_lat, pad_rows):
    grid_rows = n_lat // GRID_W
    half = HEAD_DIM // 2
    row = jnp.repeat(jnp.arange(grid_rows), GRID_W).astype(F32)
    col = jnp.tile(jnp.arange(GRID_W), grid_rows).astype(F32)
    inv = ROPE_BASE ** (-jnp.arange(0, half, 2, dtype=F32) / half)
    ang = jnp.concatenate([row[:, None] * inv, col[:, None] * inv], axis=-1)
    cos, sin = jnp.cos(ang), jnp.sin(ang)
    reps = LANES // HEAD_DIM
    cos_t = jnp.tile(jnp.concatenate([cos, cos], axis=-1), (1, reps))
    sin_t = jnp.tile(jnp.concatenate([-sin, sin], axis=-1), (1, reps))
    cos_t = jnp.concatenate([cos_t, jnp.ones((pad_rows, LANES), F32)], axis=0)
    sin_t = jnp.concatenate([sin_t, jnp.zeros((pad_rows, LANES), F32)], axis=0)
    return cos_t, sin_t


def kernel(x, c, ctx, c_ctx, w_ada, b_ada, g_mix, g_mlp, w_in, w_pool, s_pool,
           lam_q1, lam_k1, lam_q2, lam_k2, g_subln, w_out, w_mlp1, w_mlp2, g_final):
    batch, n_lat, d_model = x.shape
    n_ctx = ctx.shape[1]
    depth = w_ada.shape[0]
    rows = _Rows(batch, n_lat, n_ctx)
    assert d_model == D_MODEL and batch < ADA_ROWS
    assert n_lat % ROW_TILE == 0 and rows.ctx % ROW_TILE == 0 and n_lat % n_ctx == 0
    assert n_lat % KV_CHUNK == 0 and n_lat % GRID_W == 0 and n_lat % Q_TILE == 0
    assert n_ctx % POOL_PIECE == 0 and ROW_TILE % POST_TILE == 0 and POST_TILE % POOL_PIECE == 0

    cvec = jnp.zeros((ADA_ROWS, D_MODEL), F32).at[:batch].set(c).at[batch].set(c_ctx)
    lam_init = [0.8 - 0.6 * math.exp(-0.3 * l) for l in range(depth)]
    linit = jnp.broadcast_to(jnp.asarray(lam_init, F32)[:, None, None], (depth, 1, LANES))
    vec3 = lambda a: a.reshape(depth, 1, a.shape[-1])
    mods, lam = _ada_call(cvec, w_ada, vec3(b_ada), vec3(lam_q1), vec3(lam_k1),
                          vec3(lam_q2), vec3(lam_k2), linit)
    mods = mods.reshape(depth, ADA_ROWS, N_ADA, D_MODEL)
    scal = jnp.stack([lam[:, 0, 0], 1.0 - jnp.asarray(lam_init, F32)], axis=-1)

    cos, sin = _rope_tables(n_lat, ROW_TILE)
    g_mix3, g_mlp3, g_sub3 = vec3(g_mix), vec3(g_mlp), vec3(g_subln)
    w_mix_b = _fold_call(w_pool, vec3(s_pool), w_out)

    streams = [x.reshape(rows.lat, D_MODEL), ctx.reshape(rows.ctx, D_MODEL)]
    u, q, k, v = _inproj_call(streams, rows, 0, mods, g_mix3, w_in[0].astype(BF16), cos, sin)
    for l in range(depth):
        with_ctx = l < depth - 1
        casts = [(w_mlp1, l), (w_mlp2, l)] + ([(w_in, l + 1)] if with_ctx else [])
        att, att_ctx, (w1_b, w2_b, *w_in_next) = _attn_call(
            scal, q, k, v, g_sub3, rows, l, with_ctx, casts)
        if with_ctx:
            xs, u, q, k, v = _post_call(streams, [att, att_ctx], u, rows, l, mods, w_mix_b, g_mlp3,
                                        w1_b, w2_b, next_in=(g_mix3, w_in_next[0], cos, sin))
        else:
            xs, = _post_call(streams, [att], u, rows, l, mods, w_mix_b, g_mlp3, w1_b, w2_b,
                             g_final=g_final[None])
        streams = [xs]
    return xs.reshape(batch, n_lat, D_MODEL)
```

```python
import functools
import math

import jax
import jax.numpy as jnp
from jax import lax
from jax.experimental import pallas as pl
from jax.experimental.pallas import tpu as pltpu

D_MODEL = 1024
GRID_W = 64
POOL_WIDTH = 512
POOL_WINDOWS = (2, 4, 8, 16)
POOL_GC = 128
N_HEADS = 4
HEAD_DIM = 64
V_DIM = 128
QK_WIDTH = 512
ATTN_WIDTH = 512
IN_WIDTH = 2048
ROPE_BASE = 10000.0
D_FF = 4096
N_ADA = 6
EPS = 1e-6

LANES = 128
ADA_ROWS = 8
ROW_TILE = 1024
POST_TILE = 512
Q_TILE = 4096
POOL_PIECE = 256
HALO = 16
KV_CHUNK = 256
FF_CHUNK = 1024
ADA_COLS = 2048
VMEM_LIMIT = 56 * 1024 * 1024

F32 = jnp.float32
BF16 = jnp.bfloat16


def _params(*sem):
    return pltpu.CompilerParams(dimension_semantics=sem, vmem_limit_bytes=VMEM_LIMIT)


class _Rows:
    def __init__(self, batch, n_lat, n_ctx):
        self.batch, self.n_lat, self.n_ctx = batch, n_lat, n_ctx
        self.lat = batch * n_lat
        self.ctx = batch * n_ctx
        self.all = self.lat + self.ctx

    def mod_spec(self, layer, tile):
        lat_tiles, per_batch = self.lat // tile, self.n_lat // tile
        return pl.BlockSpec(
            (None, None, N_ADA, D_MODEL),
            lambda t: (layer, jnp.where(t < lat_tiles, t // per_batch, self.batch), 0, 0))

    def stream_specs(self, tile, width, split):
        if not split:
            return [pl.BlockSpec((tile, width), lambda t: (t, 0))]
        lat_tiles = self.lat // tile
        return [pl.BlockSpec((tile, width), lambda t: (jnp.minimum(t, lat_tiles - 1), 0)),
                pl.BlockSpec((tile, width), lambda t: (jnp.maximum(t - lat_tiles, 0), 0))]


def _stream_rows(refs, lat_tiles):
    if len(refs) == 1:
        return refs[0][...]
    return jnp.where(pl.program_id(0) >= lat_tiles, refs[1][...], refs[0][...])


def _layer_spec(layer, *shape):
    return pl.BlockSpec((None,) + shape, lambda t: (layer,) + (0,) * len(shape))


def _norm_mod(x, g, shift, scale):
    ms = jnp.mean(x * x, axis=-1, keepdims=True)
    return (x * lax.rsqrt(ms + EPS) * g) * (1.0 + scale) + shift


def _split_dot(a, b):
    a_hi = a.astype(BF16)
    a_lo = (a - a_hi.astype(F32)).astype(BF16)
    b_hi = b.astype(BF16)
    b_lo = (b - b_hi.astype(F32)).astype(BF16)
    rows = a.shape[0]
    both = jnp.dot(jnp.concatenate([a_hi, a_lo], axis=0), b_hi, preferred_element_type=F32)
    return both[:rows] + both[rows:] + jnp.dot(a_hi, b_lo, preferred_element_type=F32)


def _ada_kernel(c_ref, w_ref, b_ref, lq1_ref, lk1_ref, lq2_ref, lk2_ref, linit_ref,
                mod_ref, lam_ref):
    c = c_ref[...]
    s = c / (1.0 + jnp.exp(-c))
    mod_ref[0] = _split_dot(s, w_ref[0]) + b_ref[0]
    d1 = jnp.sum(lq1_ref[0] * lk1_ref[0], axis=-1, keepdims=True)
    d2 = jnp.sum(lq2_ref[0] * lk2_ref[0], axis=-1, keepdims=True)
    lam = jnp.exp(d1) - jnp.exp(d2) + linit_ref[0]
    lam_ref[0] = jnp.broadcast_to(lam, (ADA_ROWS, LANES))


def _ada_call(cvec, w_ada, b_ada, lq1, lk1, lq2, lk2, linit):
    depth = w_ada.shape[0]
    width = w_ada.shape[2]
    vec = lambda: pl.BlockSpec((1, 1, HEAD_DIM), lambda l, j: (l, 0, 0))
    return pl.pallas_call(
        _ada_kernel,
        grid=(depth, width // ADA_COLS),
        in_specs=[
            pl.BlockSpec((ADA_ROWS, D_MODEL), lambda l, j: (0, 0)),
            pl.BlockSpec((1, D_MODEL, ADA_COLS), lambda l, j: (l, 0, j)),
            pl.BlockSpec((1, 1, ADA_COLS), lambda l, j: (l, 0, j)),
            vec(), vec(), vec(), vec(),
            pl.BlockSpec((1, 1, LANES), lambda l, j: (l, 0, 0)),
        ],
        out_specs=[
            pl.BlockSpec((1, ADA_ROWS, ADA_COLS), lambda l, j: (l, 0, j)),
            pl.BlockSpec((1, ADA_ROWS, LANES), lambda l, j: (l, 0, 0)),
        ],
        out_shape=[
            jax.ShapeDtypeStruct((depth, ADA_ROWS, width), F32),
            jax.ShapeDtypeStruct((depth, ADA_ROWS, LANES), F32),
        ],
        compiler_params=_params("arbitrary", "arbitrary"),
        name="ada",
    )(cvec, w_ada, b_ada, lq1, lk1, lq2, lk2, linit)


def _project_in(x, mod_ref, g_ref, w_ref, cos_ref, sin_ref, u_ref, q_ref, k_ref, v_ref):
    h = _norm_mod(x, g_ref[...], mod_ref[0:1, :], mod_ref[1:2, :]).astype(BF16)
    cos = cos_ref[...]
    sin = sin_ref[...]
    lane = lax.broadcasted_iota(jnp.int32, cos.shape, 1)
    first_half = (lane & (HEAD_DIM // 2)) == 0
    q_scale = HEAD_DIM ** -0.5 * math.log2(math.e)

    def rope(t):
        partner = jnp.where(first_half,
                            pltpu.roll(t, LANES - HEAD_DIM // 2, 1),
                            pltpu.roll(t, HEAD_DIM // 2, 1))
        return t * cos + partner * sin

    def proj(lo, width):
        return jnp.dot(h, w_ref[:, lo:lo + width], preferred_element_type=F32)

    u_ref[...] = proj(0, POOL_WIDTH)
    z = proj(POOL_WIDTH, QK_WIDTH)
    for c in range(QK_WIDTH // LANES):
        cols = slice(c * LANES, (c + 1) * LANES)
        q_ref[:, cols] = (rope(z[:, cols]) * q_scale).astype(BF16)
    z = proj(POOL_WIDTH + QK_WIDTH, QK_WIDTH)
    for c in range(QK_WIDTH // LANES):
        cols = slice(c * LANES, (c + 1) * LANES)
        k_ref[:, cols] = rope(z[:, cols]).astype(BF16)
    z = proj(POOL_WIDTH + 2 * QK_WIDTH, ATTN_WIDTH)
    ones = jnp.ones((z.shape[0], V_DIM), BF16)
    for hd in range(N_HEADS):
        v_ref[:, 2 * hd * V_DIM:(2 * hd + 1) * V_DIM] = z[:, hd * V_DIM:(hd + 1) * V_DIM].astype(BF16)
        v_ref[:, (2 * hd + 1) * V_DIM:(2 * hd + 2) * V_DIM] = ones


def _inproj_kernel(*refs, lat_tiles):
    *x_refs, mod_ref, g_ref, w_ref, cos_ref, sin_ref, u_ref, q_ref, k_ref, v_ref = refs
    _project_in(_stream_rows(x_refs, lat_tiles), mod_ref, g_ref, w_ref, cos_ref, sin_ref,
                u_ref, q_ref, k_ref, v_ref)


def _table_spec(rows, tile):
    lat_tiles, per_batch = rows.lat // tile, rows.n_lat // tile
    return pl.BlockSpec((tile, LANES),
                        lambda t: (jnp.where(t < lat_tiles, t % per_batch, per_batch), 0))


def _mixer_input_outs(rows, tile):
    out = lambda width: pl.BlockSpec((tile, width), lambda t: (t, 0))
    shape = lambda width, dt: jax.ShapeDtypeStruct((rows.all, width), dt)
    return ([out(POOL_WIDTH), out(QK_WIDTH), out(QK_WIDTH), out(2 * ATTN_WIDTH)],
            [shape(POOL_WIDTH, F32), shape(QK_WIDTH, BF16), shape(QK_WIDTH, BF16),
             shape(2 * ATTN_WIDTH, BF16)])


def _inproj_call(streams, rows, layer, mods, g, w_in, cos, sin):
    tile = ROW_TILE
    lat_tiles = rows.lat // tile
    split = len(streams) == 2
    table = _table_spec(rows, tile)
    out_specs, out_shape = _mixer_input_outs(rows, tile)
    return pl.pallas_call(
        functools.partial(_inproj_kernel, lat_tiles=lat_tiles),
        grid=(rows.all // tile,),
        in_specs=rows.stream_specs(tile, D_MODEL, split) + [
            rows.mod_spec(layer, tile),
            _layer_spec(layer, 1, D_MODEL),
            pl.BlockSpec(w_in.shape, lambda t: (0, 0)),
            table, table,
        ],
        out_specs=out_specs,
        out_shape=out_shape,
        compiler_params=_params("arbitrary"),
        name="inproj",
    )(*streams, mods, g, w_in, cos, sin)


def _diff_attention(q, key_value_blocks, lam, gain):
    tq = q.shape[0]
    lane = lax.broadcasted_iota(jnp.int32, q.shape, 1)
    zero = jnp.zeros_like(q)
    qq = jnp.concatenate([jnp.where(lane < HEAD_DIM, q, zero),
                          jnp.where(lane >= HEAD_DIM, q, zero)], axis=0)

    def step(carry, kc, vc):
        m, acc = carry
        s = lax.dot_general(qq, kc, (((1,), (1,)), ((), ())), preferred_element_type=F32)
        m_new = jnp.maximum(m, jnp.max(s, axis=-1, keepdims=True))
        p = jnp.exp2(s - m_new).astype(BF16)
        pv = jnp.dot(p, vc, preferred_element_type=F32)
        if acc is None:
            return m_new, pv
        return m_new, jnp.exp2(m - m_new) * acc + pv

    carry = (jnp.full((2 * tq, 1), -jnp.inf, F32), None)
    for block in key_value_blocks:
        carry = step(carry, *block())
    acc = carry[1]
    o = acc[:, :V_DIM] / acc[:, V_DIM:]
    o = o[:tq] - lam * o[tq:]
    ms = jnp.mean(o * o, axis=-1, keepdims=True)
    return o * lax.rsqrt(ms + EPS) * gain


def _attn_kernel(scal_ref, q_ref, kl_ref, vl_ref, kc_ref, vc_ref, g_ref, *rest, layer):
    n_cast = len(rest) // 2
    cast_src, o_ref, cast_dst = rest[:n_cast], rest[n_cast], rest[n_cast + 1:]
    blocks = [functools.partial(lambda rows: (kl_ref[rows, :], vl_ref[rows, :]),
                                slice(c * KV_CHUNK, (c + 1) * KV_CHUNK))
              for c in range(kl_ref.shape[0] // KV_CHUNK)]
    blocks.append(lambda: (kc_ref[...], vc_ref[...]))
    o_ref[...] = _diff_attention(q_ref[...], blocks, scal_ref[layer, 0],
                                 g_ref[...] * scal_ref[layer, 1]).astype(o_ref.dtype)
    for src, dst in zip(cast_src, cast_dst):
        dst[...] = src[...].astype(dst.dtype)


def _attn_ctx_kernel(scal_ref, q_ref, k_ref, v_ref, g_ref, o_ref, *, layer):
    gain = g_ref[...] * scal_ref[layer, 1]
    for hd in range(N_HEADS):
        cols = slice(hd * V_DIM, (hd + 1) * V_DIM)
        vcols = slice(2 * hd * V_DIM, 2 * (hd + 1) * V_DIM)
        block = functools.partial(lambda c, vc: (k_ref[:, c], v_ref[:, vc]), cols, vcols)
        o_ref[:, cols] = _diff_attention(q_ref[:, cols], [block], scal_ref[layer, 0],
                                         gain).astype(o_ref.dtype)


def _attn_call(scal, q, k, v, g, rows, layer, with_ctx, casts):
    ctx_blocks = rows.lat // rows.n_ctx
    lat_tiles = rows.n_lat // Q_TILE
    steps = rows.batch * N_HEADS * lat_tiles
    smem = pl.BlockSpec(memory_space=pltpu.SMEM)
    lat = lambda width: pl.BlockSpec((rows.n_lat, width), lambda b, h, i: (b, h))
    ctx = lambda width: pl.BlockSpec((rows.n_ctx, width), lambda b, h, i: (ctx_blocks + b, h))
    gain = pl.BlockSpec((None, 1, V_DIM), lambda *_: (layer, 0, 0))
    q_lat = pl.BlockSpec((Q_TILE, V_DIM), lambda b, h, i: (b * lat_tiles + i, h))
    step = lambda b, h, i: (b * N_HEADS + h) * lat_tiles + i
    slab_in = lambda w, l: pl.BlockSpec((None, w.shape[1] // steps, w.shape[2]),
                                        lambda b, h, i: (l, step(b, h, i), 0))
    slab_out = lambda w: pl.BlockSpec((w.shape[1] // steps, w.shape[2]),
                                      lambda b, h, i: (step(b, h, i), 0))
    att, *cast_out = pl.pallas_call(
        functools.partial(_attn_kernel, layer=layer),
        grid=(rows.batch, N_HEADS, lat_tiles),
        in_specs=[smem, q_lat, lat(V_DIM), lat(2 * V_DIM), ctx(V_DIM), ctx(2 * V_DIM), gain]
        + [slab_in(w, l) for w, l in casts],
        out_specs=[q_lat] + [slab_out(w) for w, _ in casts],
        out_shape=[jax.ShapeDtypeStruct((rows.lat, ATTN_WIDTH), BF16)]
        + [jax.ShapeDtypeStruct(w.shape[1:], BF16) for w, _ in casts],
        compiler_params=_params("arbitrary", "arbitrary", "arbitrary"),
        name="attn",
    )(scal, q, k, v, k, v, g, *[w for w, _ in casts])
    if not with_ctx:
        return att, None, cast_out
    att_ctx = pl.pallas_call(
        functools.partial(_attn_ctx_kernel, layer=layer),
        grid=(rows.batch,),
        in_specs=[smem] + [pl.BlockSpec((rows.n_ctx, width), lambda b: (ctx_blocks + b, 0))
                           for width in (QK_WIDTH, QK_WIDTH, 2 * ATTN_WIDTH)] + [gain],
        out_specs=pl.BlockSpec((rows.n_ctx, ATTN_WIDTH), lambda b: (b, 0)),
        out_shape=jax.ShapeDtypeStruct((rows.ctx, ATTN_WIDTH), BF16),
        compiler_params=_params("arbitrary"),
        name="attn_ctx",
    )(scal, q, k, v, g)
    return att, att_ctx, cast_out


def _fold_kernel(wp_ref, sp_ref, wo_ref, o_ref):
    for gi in range(POOL_WIDTH // POOL_GC):
        rows = slice(gi * POOL_GC, (gi + 1) * POOL_GC)
        o_ref[rows, :] = _split_dot(wp_ref[gi] * sp_ref[:, rows], wo_ref[rows, :]).astype(o_ref.dtype)
    o_ref[POOL_WIDTH:, :] = wo_ref[POOL_WIDTH:, :].astype(o_ref.dtype)


def _fold_call(w_pool, s_pool, w_out):
    depth = w_out.shape[0]
    layer = lambda a: pl.BlockSpec((None,) + a.shape[1:], lambda l: (l,) + (0,) * (a.ndim - 1))
    return pl.pallas_call(
        _fold_kernel,
        grid=(depth,),
        in_specs=[layer(w_pool), layer(s_pool), layer(w_out)],
        out_specs=layer(w_out),
        out_shape=jax.ShapeDtypeStruct(w_out.shape, BF16),
        compiler_params=_params("arbitrary"),
        name="fold",
    )(w_pool, s_pool, w_out)


def _window_sum(ext, w):
    n = ext.shape[0]
    back = lambda a, k: pltpu.roll(a, n - k, 0)
    a, span = ext, 1
    while 2 * span < w:
        a = a + back(a, span)
        span *= 2
    half = w // 2
    tot = pltpu.roll(a, half, 0) + a
    return tot[HALO:n - HALO]


def _post_kernel(*refs, rows, tile, n_x, n_att, final):
    x_refs, att_refs, refs = refs[:n_x], refs[n_x:n_x + n_att], refs[n_x + n_att:]
    u_ref, up_ref, un_ref, mod_ref, wm_ref, g_ref, w1_ref, w2_ref, *refs = refs
    if final:
        gf_ref, o_ref, y_ref = refs
    else:
        *next_in, o_ref, u2_ref, q2_ref, k2_ref, v2_ref, y_ref = refs
    lat_tiles = rows.lat // tile
    r0 = pl.program_id(0) * tile
    in_ctx = r0 >= rows.lat
    seg_len = jnp.where(in_ctx, rows.n_ctx, rows.n_lat)
    seg_base = jnp.where(in_ctx, rows.lat, 0)
    pieces = tile // POOL_PIECE
    for j in range(pieces):
        lo, hi = j * POOL_PIECE, (j + 1) * POOL_PIECE
        pos = lax.rem(r0 + lo - seg_base, seg_len)
        before = up_ref[...] if j == 0 else u_ref[lo - HALO:lo, :]
        after = un_ref[...] if j == pieces - 1 else u_ref[hi:hi + HALO, :]
        before = jnp.where(pos > 0, before, 0.0)
        after = jnp.where(pos + POOL_PIECE < seg_len, after, 0.0)
        cur = u_ref[lo:hi, :]
        ext = jnp.concatenate([before, cur, after], axis=0)
        trow = pos + lax.broadcasted_iota(jnp.int32, (POOL_PIECE, 1), 0)
        for gi, w in enumerate(POOL_WINDOWS):
            cols = slice(gi * POOL_GC, (gi + 1) * POOL_GC)
            cnt = jnp.minimum(trow + (w - w // 2), seg_len) - jnp.maximum(trow - w // 2, 0)
            y = _window_sum(ext[:, cols], w) / cnt.astype(F32) - cur[:, cols]
            y_ref[lo:hi, cols] = y.astype(BF16)
    mix = jnp.dot(y_ref[...], wm_ref[:POOL_WIDTH, :], preferred_element_type=F32)
    mix += jnp.dot(_stream_rows(att_refs, lat_tiles), wm_ref[POOL_WIDTH:, :],
                   preferred_element_type=F32)
    x = _stream_rows(x_refs, lat_tiles) + mod_ref[2:3, :] * mix

    h = _norm_mod(x, g_ref[...], mod_ref[3:4, :], mod_ref[4:5, :]).astype(BF16)
    acc = jnp.zeros(x.shape, F32)
    for c in range(D_FF // FF_CHUNK):
        cols = slice(c * FF_CHUNK, (c + 1) * FF_CHUNK)
        a = jnp.dot(h, w1_ref[:, cols], preferred_element_type=F32)
        a = jnp.square(jnp.maximum(a, 0.0)).astype(BF16)
        acc += jnp.dot(a, w2_ref[cols, :], preferred_element_type=F32)
    x = x + mod_ref[5:6, :] * acc
    if final:
        ms = jnp.mean(x * x, axis=-1, keepdims=True)
        x = x * lax.rsqrt(ms + EPS) * gf_ref[...]
    o_ref[...] = x
    if not final:
        _project_in(x, *next_in, u2_ref, q2_ref, k2_ref, v2_ref)


def _post_call(streams, att_streams, u, rows, layer, mods, w_mix, g, w1, w2, g_final=None,
               next_in=None):
    tile = POST_TILE
    final = g_final is not None
    n_rows = rows.lat if final else rows.all
    halo_per_tile = tile // HALO
    last_halo = rows.all // HALO - 1
    full = lambda width: pl.BlockSpec((tile, width), lambda t: (t, 0))
    whole = lambda w: pl.BlockSpec(w.shape, lambda t: (0,) * w.ndim, pipeline_mode=pl.Buffered(1))
    out_specs, out_shape = [full(D_MODEL)], [jax.ShapeDtypeStruct((n_rows, D_MODEL), F32)]
    if final:
        extra, extra_specs = [g_final], [pl.BlockSpec((1, D_MODEL), lambda t: (0, 0))]
    else:
        g_next, w_in, cos, sin = next_in
        table = _table_spec(rows, tile)
        extra = [mods, g_next, w_in, cos, sin]
        extra_specs = [rows.mod_spec(layer + 1, tile), _layer_spec(layer + 1, 1, D_MODEL),
                       whole(w_in), table, table]
        next_specs, next_shape = _mixer_input_outs(rows, tile)
        out_specs, out_shape = out_specs + next_specs, out_shape + next_shape
    return pl.pallas_call(
        functools.partial(_post_kernel, rows=rows, tile=tile, n_x=len(streams),
                          n_att=len(att_streams), final=final),
        grid=(n_rows // tile,),
        in_specs=rows.stream_specs(tile, D_MODEL, len(streams) == 2)
        + rows.stream_specs(tile, ATTN_WIDTH, len(att_streams) == 2) + [
            full(POOL_WIDTH),
            pl.BlockSpec((HALO, POOL_WIDTH), lambda t: (jnp.maximum(t * halo_per_tile - 1, 0), 0)),
            pl.BlockSpec((HALO, POOL_WIDTH),
                         lambda t: (jnp.minimum((t + 1) * halo_per_tile, last_halo), 0)),
            rows.mod_spec(layer, tile),
            pl.BlockSpec((None,) + w_mix.shape[1:], lambda t: (layer, 0, 0),
                         pipeline_mode=pl.Buffered(1)),
            _layer_spec(layer, 1, D_MODEL),
            whole(w1), whole(w2),
        ] + extra_specs,
        out_specs=out_specs,
        out_shape=out_shape,
        scratch_shapes=[pltpu.VMEM((tile, POOL_WIDTH), BF16)],
        compiler_params=_params("arbitrary"),
        name="post",
    )(*streams, *att_streams, u, u, u, mods, w_mix, g, w1, w2, *extra)


def _rope_tables(n_lat, pad_rows):
    grid_rows = n_lat // GRID_W
    half = HEAD_DIM // 2
    row = jnp.repeat(jnp.arange(grid_rows), GRID_W).astype(F32)
    col = jnp.tile(jnp.arange(GRID_W), grid_rows).astype(F32)
    inv = ROPE_BASE ** (-jnp.arange(0, half, 2, dtype=F32) / half)
    ang = jnp.concatenate([row[:, None] * inv, col[:, None] * inv], axis=-1)
    cos, sin = jnp.cos(ang), jnp.sin(ang)
    reps = LANES // HEAD_DIM
    cos_t = jnp.tile(jnp.concatenate([cos, cos], axis=-1), (1, reps))
    sin_t = jnp.tile(jnp.concatenate([-sin, sin], axis=-1), (1, reps))
    cos_t = jnp.concatenate([cos_t, jnp.ones((pad_rows, LANES), F32)], axis=0)
    sin_t = jnp.concatenate([sin_t, jnp.zeros((pad_rows, LANES), F32)], axis=0)
    return cos_t, sin_t


def kernel(x, c, ctx, c_ctx, w_ada, b_ada, g_mix, g_mlp, w_in, w_pool, s_pool,
           lam_q1, lam_k1, lam_q2, lam_k2, g_subln, w_out, w_mlp1, w_mlp2, g_final):
    batch, n_lat, d_model = x.shape
    n_ctx = ctx.shape[1]
    depth = w_ada.shape[0]
    rows = _Rows(batch, n_lat, n_ctx)
    assert d_model == D_MODEL and batch < ADA_ROWS
    assert n_lat % ROW_TILE == 0 and rows.ctx % ROW_TILE == 0 and n_lat % n_ctx == 0
    assert n_lat % KV_CHUNK == 0 and n_lat % GRID_W == 0 and n_lat % Q_TILE == 0
    assert n_ctx % POOL_PIECE == 0 and ROW_TILE % POST_TILE == 0 and POST_TILE % POOL_PIECE == 0

    cvec = jnp.zeros((ADA_ROWS, D_MODEL), F32).at[:batch].set(c).at[batch].set(c_ctx)
    lam_init = [0.8 - 0.6 * math.exp(-0.3 * l) for l in range(depth)]
    linit = jnp.broadcast_to(jnp.asarray(lam_init, F32)[:, None, None], (depth, 1, LANES))
    vec3 = lambda a: a.reshape(depth, 1, a.shape[-1])
    mods, lam = _ada_call(cvec, w_ada, vec3(b_ada), vec3(lam_q1), vec3(lam_k1),
                          vec3(lam_q2), vec3(lam_k2), linit)
    mods = mods.reshape(depth, ADA_ROWS, N_ADA, D_MODEL)
    scal = jnp.stack([lam[:, 0, 0], 1.0 - jnp.asarray(lam_init, F32)], axis=-1)

    cos, sin = _rope_tables(n_lat, ROW_TILE)
    g_mix3, g_mlp3, g_sub3 = vec3(g_mix), vec3(g_mlp), vec3(g_subln)
    w_mix_b = _fold_call(w_pool, vec3(s_pool), w_out)

    streams = [x.reshape(rows.lat, D_MODEL), ctx.reshape(rows.ctx, D_MODEL)]
    u, q, k, v = _inproj_call(streams, rows, 0, mods, g_mix3, w_in[0].astype(BF16), cos, sin)
    for l in range(depth):
        with_ctx = l < depth - 1
        casts = [(w_mlp1, l), (w_mlp2, l)] + ([(w_in, l + 1)] if with_ctx else [])
        att, att_ctx, (w1_b, w2_b, *w_in_next) = _attn_call(
            scal, q, k, v, g_sub3, rows, l, with_ctx, casts)
        if with_ctx:
            xs, u, q, k, v = _post_call(streams, [att, att_ctx], u, rows, l, mods, w_mix_b, g_mlp3,
                                        w1_b, w2_b, next_in=(g_mix3, w_in_next[0], cos, sin))
        else:
            xs, = _post_call(streams, [att], u, rows, l, mods, w_mix_b, g_mlp3, w1_b, w2_b,
                             g_final=g_final[None])
        streams = [xs]
    return xs.reshape(batch, n_lat, D_MODEL)
```

```python
import functools
import math

import jax
import jax.numpy as jnp
from jax import lax
from jax.experimental import pallas as pl
from jax.experimental.pallas import tpu as pltpu

D_MODEL = 1024
GRID_W = 64
POOL_WIDTH = 512
POOL_WINDOWS = (2, 4, 8, 16)
POOL_GC = 128
N_HEADS = 4
HEAD_DIM = 64
V_DIM = 128
QK_WIDTH = 512
ATTN_WIDTH = 512
IN_WIDTH = 2048
ROPE_BASE = 10000.0
D_FF = 4096
N_ADA = 6
EPS = 1e-6

LANES = 128
ADA_ROWS = 8
ROW_TILE = 1024
POST_TILE = 512
Q_TILE = 2048
POOL_PIECE = 256
HALO = 16
KV_CHUNK = 512
FF_CHUNK = 1024
ADA_COLS = 2048
VMEM_LIMIT = 56 * 1024 * 1024

F32 = jnp.float32
BF16 = jnp.bfloat16


def _params(*sem):
    return pltpu.CompilerParams(dimension_semantics=sem, vmem_limit_bytes=VMEM_LIMIT)


class _Rows:
    def __init__(self, batch, n_lat, n_ctx):
        self.batch, self.n_lat, self.n_ctx = batch, n_lat, n_ctx
        self.lat = batch * n_lat
        self.ctx = batch * n_ctx
        self.all = self.lat + self.ctx

    def mod_spec(self, layer, tile):
        lat_tiles, per_batch = self.lat // tile, self.n_lat // tile
        return pl.BlockSpec(
            (None, None, N_ADA, D_MODEL),
            lambda t: (layer, jnp.where(t < lat_tiles, t // per_batch, self.batch), 0, 0))

    def stream_specs(self, tile, width, split):
        if not split:
            return [pl.BlockSpec((tile, width), lambda t: (t, 0))]
        lat_tiles = self.lat // tile
        return [pl.BlockSpec((tile, width), lambda t: (jnp.minimum(t, lat_tiles - 1), 0)),
                pl.BlockSpec((tile, width), lambda t: (jnp.maximum(t - lat_tiles, 0), 0))]


def _stream_rows(refs, lat_tiles):
    if len(refs) == 1:
        return refs[0][...]
    return jnp.where(pl.program_id(0) >= lat_tiles, refs[1][...], refs[0][...])


def _layer_spec(layer, *shape):
    return pl.BlockSpec((None,) + shape, lambda t: (layer,) + (0,) * len(shape))


def _norm_mod(x, g, shift, scale):
    ms = jnp.mean(x * x, axis=-1, keepdims=True)
    return (x * lax.rsqrt(ms + EPS) * g) * (1.0 + scale) + shift


def _split_dot(a, b):
    a_hi = a.astype(BF16)
    a_lo = (a - a_hi.astype(F32)).astype(BF16)
    b_hi = b.astype(BF16)
    b_lo = (b - b_hi.astype(F32)).astype(BF16)
    rows = a.shape[0]
    both = jnp.dot(jnp.concatenate([a_hi, a_lo], axis=0), b_hi, preferred_element_type=F32)
    return both[:rows] + both[rows:] + jnp.dot(a_hi, b_lo, preferred_element_type=F32)


def _ada_kernel(c_ref, w_ref, b_ref, lq1_ref, lk1_ref, lq2_ref, lk2_ref, linit_ref,
                mod_ref, lam_ref):
    c = c_ref[...]
    s = c / (1.0 + jnp.exp(-c))
    mod_ref[0] = _split_dot(s, w_ref[0]) + b_ref[0]
    d1 = jnp.sum(lq1_ref[0] * lk1_ref[0], axis=-1, keepdims=True)
    d2 = jnp.sum(lq2_ref[0] * lk2_ref[0], axis=-1, keepdims=True)
    lam = jnp.exp(d1) - jnp.exp(d2) + linit_ref[0]
    lam_ref[0] = jnp.broadcast_to(lam, (ADA_ROWS, LANES))


def _ada_call(cvec, w_ada, b_ada, lq1, lk1, lq2, lk2, linit):
    depth = w_ada.shape[0]
    width = w_ada.shape[2]
    vec = lambda: pl.BlockSpec((1, 1, HEAD_DIM), lambda l, j: (l, 0, 0))
    return pl.pallas_call(
        _ada_kernel,
        grid=(depth, width // ADA_COLS),
        in_specs=[
            pl.BlockSpec((ADA_ROWS, D_MODEL), lambda l, j: (0, 0)),
            pl.BlockSpec((1, D_MODEL, ADA_COLS), lambda l, j: (l, 0, j)),
            pl.BlockSpec((1, 1, ADA_COLS), lambda l, j: (l, 0, j)),
            vec(), vec(), vec(), vec(),
            pl.BlockSpec((1, 1, LANES), lambda l, j: (l, 0, 0)),
        ],
        out_specs=[
            pl.BlockSpec((1, ADA_ROWS, ADA_COLS), lambda l, j: (l, 0, j)),
            pl.BlockSpec((1, ADA_ROWS, LANES), lambda l, j: (l, 0, 0)),
        ],
        out_shape=[
            jax.ShapeDtypeStruct((depth, ADA_ROWS, width), F32),
            jax.ShapeDtypeStruct((depth, ADA_ROWS, LANES), F32),
        ],
        compiler_params=_params("arbitrary", "arbitrary"),
        name="ada",
    )(cvec, w_ada, b_ada, lq1, lk1, lq2, lk2, linit)


def _project_in(x, mod_ref, g_ref, w_ref, cos_ref, sin_ref, u_ref, q_ref, k_ref, v_ref):
    h = _norm_mod(x, g_ref[...], mod_ref[0:1, :], mod_ref[1:2, :]).astype(BF16)
    cos = cos_ref[...]
    sin = sin_ref[...]
    lane = lax.broadcasted_iota(jnp.int32, cos.shape, 1)
    first_half = (lane & (HEAD_DIM // 2)) == 0
    q_scale = HEAD_DIM ** -0.5 * math.log2(math.e)

    def rope(t):
        partner = jnp.where(first_half,
                            pltpu.roll(t, LANES - HEAD_DIM // 2, 1),
                            pltpu.roll(t, HEAD_DIM // 2, 1))
        return t * cos + partner * sin

    def proj(lo, width):
        return jnp.dot(h, w_ref[:, lo:lo + width], preferred_element_type=F32)

    u_ref[...] = proj(0, POOL_WIDTH)
    z = proj(POOL_WIDTH, QK_WIDTH)
    for c in range(QK_WIDTH // LANES):
        cols = slice(c * LANES, (c + 1) * LANES)
        q_ref[:, cols] = (rope(z[:, cols]) * q_scale).astype(BF16)
    z = proj(POOL_WIDTH + QK_WIDTH, QK_WIDTH)
    for c in range(QK_WIDTH // LANES):
        cols = slice(c * LANES, (c + 1) * LANES)
        k_ref[:, cols] = rope(z[:, cols]).astype(BF16)
    z = proj(POOL_WIDTH + 2 * QK_WIDTH, ATTN_WIDTH)
    ones = jnp.ones((z.shape[0], V_DIM), BF16)
    for hd in range(N_HEADS):
        v_ref[:, 2 * hd * V_DIM:(2 * hd + 1) * V_DIM] = z[:, hd * V_DIM:(hd + 1) * V_DIM].astype(BF16)
        v_ref[:, (2 * hd + 1) * V_DIM:(2 * hd + 2) * V_DIM] = ones


def _inproj_kernel(*refs, lat_tiles):
    *x_refs, mod_ref, g_ref, w_ref, cos_ref, sin_ref, u_ref, q_ref, k_ref, v_ref = refs
    _project_in(_stream_rows(x_refs, lat_tiles), mod_ref, g_ref, w_ref, cos_ref, sin_ref,
                u_ref, q_ref, k_ref, v_ref)


def _table_spec(rows, tile):
    lat_tiles, per_batch = rows.lat // tile, rows.n_lat // tile
    return pl.BlockSpec((tile, LANES),
                        lambda t: (jnp.where(t < lat_tiles, t % per_batch, per_batch), 0))


def _mixer_input_outs(rows, tile):
    out = lambda width: pl.BlockSpec((tile, width), lambda t: (t, 0))
    shape = lambda width, dt: jax.ShapeDtypeStruct((rows.all, width), dt)
    return ([out(POOL_WIDTH), out(QK_WIDTH), out(QK_WIDTH), out(2 * ATTN_WIDTH)],
            [shape(POOL_WIDTH, F32), shape(QK_WIDTH, BF16), shape(QK_WIDTH, BF16),
             shape(2 * ATTN_WIDTH, BF16)])


def _inproj_call(streams, rows, layer, mods, g, w_in, cos, sin):
    tile = ROW_TILE
    lat_tiles = rows.lat // tile
    split = len(streams) == 2
    table = _table_spec(rows, tile)
    out_specs, out_shape = _mixer_input_outs(rows, tile)
    return pl.pallas_call(
        functools.partial(_inproj_kernel, lat_tiles=lat_tiles),
        grid=(rows.all // tile,),
        in_specs=rows.stream_specs(tile, D_MODEL, split) + [
            rows.mod_spec(layer, tile),
            _layer_spec(layer, 1, D_MODEL),
            pl.BlockSpec(w_in.shape, lambda t: (0, 0)),
            table, table,
        ],
        out_specs=out_specs,
        out_shape=out_shape,
        compiler_params=_params("arbitrary"),
        name="inproj",
    )(*streams, mods, g, w_in, cos, sin)


def _diff_attention(q, key_value_blocks, lam, gain):
    tq = q.shape[0]
    lane = lax.broadcasted_iota(jnp.int32, q.shape, 1)
    zero = jnp.zeros_like(q)
    qq = jnp.concatenate([jnp.where(lane < HEAD_DIM, q, zero),
                          jnp.where(lane >= HEAD_DIM, q, zero)], axis=0)

    def step(carry, kc, vc):
        m, acc = carry
        s = lax.dot_general(qq, kc, (((1,), (1,)), ((), ())), preferred_element_type=F32)
        m_new = jnp.maximum(m, jnp.max(s, axis=-1, keepdims=True))
        p = jnp.exp2(s - m_new).astype(BF16)
        pv = jnp.dot(p, vc, preferred_element_type=F32)
        if acc is None:
            return m_new, pv
        return m_new, jnp.exp2(m - m_new) * acc + pv

    carry = (jnp.full((2 * tq, 1), -jnp.inf, F32), None)
    for block in key_value_blocks:
        carry = step(carry, *block())
    acc = carry[1]
    o = acc[:, :V_DIM] / acc[:, V_DIM:]
    o = o[:tq] - lam * o[tq:]
    ms = jnp.mean(o * o, axis=-1, keepdims=True)
    return o * lax.rsqrt(ms + EPS) * gain


def _attn_kernel(scal_ref, q_ref, kl_ref, vl_ref, kc_ref, vc_ref, g_ref, *rest, layer):
    n_cast = len(rest) // 2
    cast_src, o_ref, cast_dst = rest[:n_cast], rest[n_cast], rest[n_cast + 1:]
    blocks = [functools.partial(lambda rows: (kl_ref[rows, :], vl_ref[rows, :]),
                                slice(c * KV_CHUNK, (c + 1) * KV_CHUNK))
              for c in range(kl_ref.shape[0] // KV_CHUNK)]
    blocks.append(lambda: (kc_ref[...], vc_ref[...]))
    o_ref[...] = _diff_attention(q_ref[...], blocks, scal_ref[layer, 0],
                                 g_ref[...] * scal_ref[layer, 1]).astype(o_ref.dtype)
    for src, dst in zip(cast_src, cast_dst):
        dst[...] = src[...].astype(dst.dtype)


def _attn_ctx_kernel(scal_ref, q_ref, k_ref, v_ref, g_ref, o_ref, *, layer):
    gain = g_ref[...] * scal_ref[layer, 1]
    for hd in range(N_HEADS):
        cols = slice(hd * V_DIM, (hd + 1) * V_DIM)
        vcols = slice(2 * hd * V_DIM, 2 * (hd + 1) * V_DIM)
        block = functools.partial(lambda c, vc: (k_ref[:, c], v_ref[:, vc]), cols, vcols)
        o_ref[:, cols] = _diff_attention(q_ref[:, cols], [block], scal_ref[layer, 0],
                                         gain).astype(o_ref.dtype)


def _attn_call(scal, q, k, v, g, rows, layer, with_ctx, casts):
    ctx_blocks = rows.lat // rows.n_ctx
    lat_tiles = rows.n_lat // Q_TILE
    steps = rows.batch * N_HEADS * lat_tiles
    smem = pl.BlockSpec(memory_space=pltpu.SMEM)
    lat = lambda width: pl.BlockSpec((rows.n_lat, width), lambda b, h, i: (b, h))
    ctx = lambda width: pl.BlockSpec((rows.n_ctx, width), lambda b, h, i: (ctx_blocks + b, h))
    gain = pl.BlockSpec((None, 1, V_DIM), lambda *_: (layer, 0, 0))
    q_lat = pl.BlockSpec((Q_TILE, V_DIM), lambda b, h, i: (b * lat_tiles + i, h))
    step = lambda b, h, i: (b * N_HEADS + h) * lat_tiles + i
    slab_in = lambda w, l: pl.BlockSpec((None, w.shape[1] // steps, w.shape[2]),
                                        lambda b, h, i: (l, step(b, h, i), 0))
    slab_out = lambda w: pl.BlockSpec((w.shape[1] // steps, w.shape[2]),
                                      lambda b, h, i: (step(b, h, i), 0))
    att, *cast_out = pl.pallas_call(
        functools.partial(_attn_kernel, layer=layer),
        grid=(rows.batch, N_HEADS, lat_tiles),
        in_specs=[smem, q_lat, lat(V_DIM), lat(2 * V_DIM), ctx(V_DIM), ctx(2 * V_DIM), gain]
        + [slab_in(w, l) for w, l in casts],
        out_specs=[q_lat] + [slab_out(w) for w, _ in casts],
        out_shape=[jax.ShapeDtypeStruct((rows.lat, ATTN_WIDTH), BF16)]
        + [jax.ShapeDtypeStruct(w.shape[1:], BF16) for w, _ in casts],
        compiler_params=_params("arbitrary", "arbitrary", "arbitrary"),
        name="attn",
    )(scal, q, k, v, k, v, g, *[w for w, _ in casts])
    if not with_ctx:
        return att, None, cast_out
    att_ctx = pl.pallas_call(
        functools.partial(_attn_ctx_kernel, layer=layer),
        grid=(rows.batch,),
        in_specs=[smem] + [pl.BlockSpec((rows.n_ctx, width), lambda b: (ctx_blocks + b, 0))
                           for width in (QK_WIDTH, QK_WIDTH, 2 * ATTN_WIDTH)] + [gain],
        out_specs=pl.BlockSpec((rows.n_ctx, ATTN_WIDTH), lambda b: (b, 0)),
        out_shape=jax.ShapeDtypeStruct((rows.ctx, ATTN_WIDTH), BF16),
        compiler_params=_params("arbitrary"),
        name="attn_ctx",
    )(scal, q, k, v, g)
    return att, att_ctx, cast_out


def _fold_kernel(wp_ref, sp_ref, wo_ref, o_ref):
    for gi in range(POOL_WIDTH // POOL_GC):
        rows = slice(gi * POOL_GC, (gi + 1) * POOL_GC)
        o_ref[rows, :] = _split_dot(wp_ref[gi] * sp_ref[:, rows], wo_ref[rows, :]).astype(o_ref.dtype)
    o_ref[POOL_WIDTH:, :] = wo_ref[POOL_WIDTH:, :].astype(o_ref.dtype)


def _fold_call(w_pool, s_pool, w_out):
    depth = w_out.shape[0]
    layer = lambda a: pl.BlockSpec((None,) + a.shape[1:], lambda l: (l,) + (0,) * (a.ndim - 1))
    return pl.pallas_call(
        _fold_kernel,
        grid=(depth,),
        in_specs=[layer(w_pool), layer(s_pool), layer(w_out)],
        out_specs=layer(w_out),
        out_shape=jax.ShapeDtypeStruct(w_out.shape, BF16),
        compiler_params=_params("arbitrary"),
        name="fold",
    )(w_pool, s_pool, w_out)


def _window_sum(ext, w):
    n = ext.shape[0]
    back = lambda a, k: pltpu.roll(a, n - k, 0)
    a, span = ext, 1
    while 2 * span < w:
        a = a + back(a, span)
        span *= 2
    half = w // 2
    tot = pltpu.roll(a, half, 0) + a
    return tot[HALO:n - HALO]


def _post_kernel(*refs, rows, tile, n_x, n_att, final):
    x_refs, att_refs, refs = refs[:n_x], refs[n_x:n_x + n_att], refs[n_x + n_att:]
    u_ref, up_ref, un_ref, mod_ref, wm_ref, g_ref, w1_ref, w2_ref, *refs = refs
    if final:
        gf_ref, o_ref, y_ref = refs
    else:
        *next_in, o_ref, u2_ref, q2_ref, k2_ref, v2_ref, y_ref = refs
    lat_tiles = rows.lat // tile
    r0 = pl.program_id(0) * tile
    in_ctx = r0 >= rows.lat
    seg_len = jnp.where(in_ctx, rows.n_ctx, rows.n_lat)
    seg_base = jnp.where(in_ctx, rows.lat, 0)
    pieces = tile // POOL_PIECE
    for j in range(pieces):
        lo, hi = j * POOL_PIECE, (j + 1) * POOL_PIECE
        pos = lax.rem(r0 + lo - seg_base, seg_len)
        before = up_ref[...] if j == 0 else u_ref[lo - HALO:lo, :]
        after = un_ref[...] if j == pieces - 1 else u_ref[hi:hi + HALO, :]
        before = jnp.where(pos > 0, before, 0.0)
        after = jnp.where(pos + POOL_PIECE < seg_len, after, 0.0)
        cur = u_ref[lo:hi, :]
        ext = jnp.concatenate([before, cur, after], axis=0)
        trow = pos + lax.broadcasted_iota(jnp.int32, (POOL_PIECE, 1), 0)
        for gi, w in enumerate(POOL_WINDOWS):
            cols = slice(gi * POOL_GC, (gi + 1) * POOL_GC)
            cnt = jnp.minimum(trow + (w - w // 2), seg_len) - jnp.maximum(trow - w // 2, 0)
            y = _window_sum(ext[:, cols], w) / cnt.astype(F32) - cur[:, cols]
            y_ref[lo:hi, cols] = y.astype(BF16)
    mix = jnp.dot(y_ref[...], wm_ref[:POOL_WIDTH, :], preferred_element_type=F32)
    mix += jnp.dot(_stream_rows(att_refs, lat_tiles), wm_ref[POOL_WIDTH:, :],
                   preferred_element_type=F32)
    x = _stream_rows(x_refs, lat_tiles) + mod_ref[2:3, :] * mix

    h = _norm_mod(x, g_ref[...], mod_ref[3:4, :], mod_ref[4:5, :]).astype(BF16)
    acc = jnp.zeros(x.shape, F32)
    for c in range(D_FF // FF_CHUNK):
        cols = slice(c * FF_CHUNK, (c + 1) * FF_CHUNK)
        a = jnp.dot(h, w1_ref[:, cols], preferred_element_type=F32)
        a = jnp.square(jnp.maximum(a, 0.0)).astype(BF16)
        acc += jnp.dot(a, w2_ref[cols, :], preferred_element_type=F32)
    x = x + mod_ref[5:6, :] * acc
    if final:
        ms = jnp.mean(x * x, axis=-1, keepdims=True)
        x = x * lax.rsqrt(ms + EPS) * gf_ref[...]
    o_ref[...] = x
    if not final:
        _project_in(x, *next_in, u2_ref, q2_ref, k2_ref, v2_ref)


def _post_call(streams, att_streams, u, rows, layer, mods, w_mix, g, w1, w2, g_final=None,
               next_in=None):
    tile = POST_TILE
    final = g_final is not None
    n_rows = rows.lat if final else rows.all
    halo_per_tile = tile // HALO
    last_halo = rows.all // HALO - 1
    full = lambda width: pl.BlockSpec((tile, width), lambda t: (t, 0))
    whole = lambda w: pl.BlockSpec(w.shape, lambda t: (0,) * w.ndim, pipeline_mode=pl.Buffered(1))
    out_specs, out_shape = [full(D_MODEL)], [jax.ShapeDtypeStruct((n_rows, D_MODEL), F32)]
    if final:
        extra, extra_specs = [g_final], [pl.BlockSpec((1, D_MODEL), lambda t: (0, 0))]
    else:
        g_next, w_in, cos, sin = next_in
        table = _table_spec(rows, tile)
        extra = [mods, g_next, w_in, cos, sin]
        extra_specs = [rows.mod_spec(layer + 1, tile), _layer_spec(layer + 1, 1, D_MODEL),
                       whole(w_in), table, table]
        next_specs, next_shape = _mixer_input_outs(rows, tile)
        out_specs, out_shape = out_specs + next_specs, out_shape + next_shape
    return pl.pallas_call(
        functools.partial(_post_kernel, rows=rows, tile=tile, n_x=len(streams),
                          n_att=len(att_streams), final=final),
        grid=(n_rows // tile,),
        in_specs=rows.stream_specs(tile, D_MODEL, len(streams) == 2)
        + rows.stream_specs(tile, ATTN_WIDTH, len(att_streams) == 2) + [
            full(POOL_WIDTH),
            pl.BlockSpec((HALO, POOL_WIDTH), lambda t: (jnp.maximum(t * halo_per_tile - 1, 0), 0)),
            pl.BlockSpec((HALO, POOL_WIDTH),
                         lambda t: (jnp.minimum((t + 1) * halo_per_tile, last_halo), 0)),
            rows.mod_spec(layer, tile),
            pl.BlockSpec((None,) + w_mix.shape[1:], lambda t: (layer, 0, 0),
                         pipeline_mode=pl.Buffered(1)),
            _layer_spec(layer, 1, D_MODEL),
            whole(w1), whole(w2),
        ] + extra_specs,
        out_specs=out_specs,
        out_shape=out_shape,
        scratch_shapes=[pltpu.VMEM((tile, POOL_WIDTH), BF16)],
        compiler_params=_params("arbitrary"),
        name="post",
    )(*streams, *att_streams, u, u, u, mods, w_mix, g, w1, w2, *extra)


def _rope_tables(n_lat, pad_rows):
    grid_rows = n_lat // GRID_W
    half = HEAD_DIM // 2
    row = jnp.repeat(jnp.arange(grid_rows), GRID_W).astype(F32)
    col = jnp.tile(jnp.arange(GRID_W), grid_rows).astype(F32)
    inv = ROPE_BASE ** (-jnp.arange(0, half, 2, dtype=F32) / half)
    ang = jnp.concatenate([row[:, None] * inv, col[:, None] * inv], axis=-1)
    cos, sin = jnp.cos(ang), jnp.sin(ang)
    reps = LANES // HEAD_DIM
    cos_t = jnp.tile(jnp.concatenate([cos, cos], axis=-1), (1, reps))
    sin_t = jnp.tile(jnp.concatenate([-sin, sin], axis=-1), (1, reps))
    cos_t = jnp.concatenate([cos_t, jnp.ones((pad_rows, LANES), F32)], axis=0)
    sin_t = jnp.concatenate([sin_t, jnp.zeros((pad_rows, LANES), F32)], axis=0)
    return cos_t, sin_t


def kernel(x, c, ctx, c_ctx, w_ada, b_ada, g_mix, g_mlp, w_in, w_pool, s_pool,
           lam_q1, lam_k1, lam_q2, lam_k2, g_subln, w_out, w_mlp1, w_mlp2, g_final):
    batch, n_lat, d_model = x.shape
    n_ctx = ctx.shape[1]
    depth = w_ada.shape[0]
    rows = _Rows(batch, n_lat, n_ctx)
    assert d_model == D_MODEL and batch < ADA_ROWS
    assert n_lat % ROW_TILE == 0 and rows.ctx % ROW_TILE == 0 and n_lat % n_ctx == 0
    assert n_lat % KV_CHUNK == 0 and n_lat % GRID_W == 0 and n_lat % Q_TILE == 0
    assert n_ctx % POOL_PIECE == 0 and ROW_TILE % POST_TILE == 0 and POST_TILE % POOL_PIECE == 0

    cvec = jnp.zeros((ADA_ROWS, D_MODEL), F32).at[:batch].set(c).at[batch].set(c_ctx)
    lam_init = [0.8 - 0.6 * math.exp(-0.3 * l) for l in range(depth)]
    linit = jnp.broadcast_to(jnp.asarray(lam_init, F32)[:, None, None], (depth, 1, LANES))
    vec3 = lambda a: a.reshape(depth, 1, a.shape[-1])
    mods, lam = _ada_call(cvec, w_ada, vec3(b_ada), vec3(lam_q1), vec3(lam_k1),
                          vec3(lam_q2), vec3(lam_k2), linit)
    mods = mods.reshape(depth, ADA_ROWS, N_ADA, D_MODEL)
    scal = jnp.stack([lam[:, 0, 0], 1.0 - jnp.asarray(lam_init, F32)], axis=-1)

    cos, sin = _rope_tables(n_lat, ROW_TILE)
    g_mix3, g_mlp3, g_sub3 = vec3(g_mix), vec3(g_mlp), vec3(g_subln)
    w_mix_b = _fold_call(w_pool, vec3(s_pool), w_out)

    streams = [x.reshape(rows.lat, D_MODEL), ctx.reshape(rows.ctx, D_MODEL)]
    u, q, k, v = _inproj_call(streams, rows, 0, mods, g_mix3, w_in[0].astype(BF16), cos, sin)
    for l in range(depth):
        with_ctx = l < depth - 1
        casts = [(w_mlp1, l), (w_mlp2, l)] + ([(w_in, l + 1)] if with_ctx else [])
        att, att_ctx, (w1_b, w2_b, *w_in_next) = _attn_call(
            scal, q, k, v, g_sub3, rows, l, with_ctx, casts)
        if with_ctx:
            xs, u, q, k, v = _post_call(streams, [att, att_ctx], u, rows, l, mods, w_mix_b, g_mlp3,
                                        w1_b, w2_b, next_in=(g_mix3, w_in_next[0], cos, sin))
        else:
            xs, = _post_call(streams, [att], u, rows, l, mods, w_mix_b, g_mlp3, w1_b, w2_b,
                             g_final=g_final[None])
        streams = [xs]
    return xs.reshape(batch, n_lat, D_MODEL)
```
